```python
import math
import jax
import jax.numpy as jnp
from jax import lax
import numpy as np

D_MODEL = 1024
BATCH = 8
SEQ = 2048
DEPTH = 4

GRID_W = 64
CTX_LEN = 256
N_MIXERS = 4
N_CYCLES = DEPTH // N_MIXERS
Q_BLOCK = 128
ROPE_THETA = 10000.0
NORM_EPS = 1e-6
NEG_INF = -1e30

A_HEADS = 8
A_HEAD_DIM = 64
B_HEADS = 16
B_KV_HEADS = 4
B_HEAD_DIM = 64
B_WINDOW = 128
C_HEADS = 8
C_Q_RANK = 384
C_KV_RANK = 256
C_NOPE_DIM = 128
C_ROPE_DIM = 64
C_V_DIM = 128
D_HEADS = 8
D_KV_HEADS = 4
D_HEAD_DIM = 128
D_FF = 2816
CONV_W = 3

kernel_name = 'hybrid_interleaved_diffusion_trunk'


def rms_norm(x, g):
    xf = x.astype(jnp.float32)
    y = xf * lax.rsqrt(jnp.mean(xf * xf, axis=-1, keepdims=True) + NORM_EPS)
    return (y * g.astype(jnp.float32)).astype(x.dtype)


def modulate(x, g, shift, scale):
    return rms_norm(x, g) * (1.0 + scale) + shift


def axial_rope_tables(n_tokens, rot_dim):
    n_rows = n_tokens // GRID_W
    rows = jnp.repeat(jnp.arange(n_rows, dtype=jnp.float32), GRID_W)
    cols = jnp.tile(jnp.arange(GRID_W, dtype=jnp.float32), n_rows)
    n_freq = rot_dim // 4
    inv_freq = ROPE_THETA ** (-jnp.arange(n_freq, dtype=jnp.float32) / n_freq)
    ang = jnp.concatenate([rows[:, None] * inv_freq, cols[:, None] * inv_freq], axis=-1)
    return jnp.cos(ang), jnp.sin(ang)


def apply_rope(x, cos, sin):
    x1, x2 = jnp.split(x, 2, axis=-1)
    c = cos[:, None, :].astype(x.dtype)
    s = sin[:, None, :].astype(x.dtype)
    return jnp.concatenate([x1 * c - x2 * s, x1 * s + x2 * c], axis=-1)


def sweep_query_blocks(fn, q):
    b, s = q.shape[:2]
    nb = s // Q_BLOCK
    qb = jnp.moveaxis(q.reshape(b, nb, Q_BLOCK, *q.shape[2:]), 1, 0)
    out = lax.map(lambda a: fn(a[0], a[1]), (jnp.arange(nb), qb))
    return jnp.moveaxis(out, 0, 1).reshape(b, s, *out.shape[3:])


def gqa_attend(q, k, v, scale, mask=None, sink=None):
    b, nq, h, dh = q.shape
    hkv = k.shape[2]
    g = h // hkv
    qg = q.reshape(b, nq, hkv, g, dh)
    s = jnp.einsum('bqhgd,bkhd->bhgqk', qg, k).astype(jnp.float32) * scale
    if mask is not None:
        s = jnp.where(mask, s, NEG_INF)
    if sink is None:
        p = jax.nn.softmax(s, axis=-1)
    else:
        sink_col = jnp.broadcast_to(sink.astype(jnp.float32).reshape(1, hkv, g, 1, 1), s.shape[:-1] + (1,))
        p = jax.nn.softmax(jnp.concatenate([s, sink_col], axis=-1), axis=-1)[..., :-1]
    o = jnp.einsum('bhgqk,bkhd->bqhgd', p.astype(v.dtype), v)
    return o.reshape(b, nq, h, v.shape[-1])


def merge_heads(o, w_o):
    return o.reshape(o.shape[0], o.shape[1], -1) @ w_o


def mixer_diff(h_ctx, h_lat, layer_idx, need_ctx, w_qkv, w_o, lq1, lk1, lq2, lk2, subln_g):
    lambda_init = 0.8 - 0.6 * math.exp(-0.3 * layer_idx)
    f32 = jnp.float32
    lam = (jnp.exp(jnp.sum(lq1.astype(f32) * lk1.astype(f32)))
           - jnp.exp(jnp.sum(lq2.astype(f32) * lk2.astype(f32))) + lambda_init)
    scale = A_HEAD_DIM ** -0.5

    def project(h):
        b, s, _ = h.shape
        q, k, v = jnp.split(h @ w_qkv, 3, axis=-1)
        return (q.reshape(b, s, 2 * A_HEADS, A_HEAD_DIM),
                k.reshape(b, s, 2 * A_HEADS, A_HEAD_DIM),
                v.reshape(b, s, A_HEADS, 2 * A_HEAD_DIM))

    def attend(q, k, v):
        s = jnp.einsum('bqhd,bkhd->bhqk', q, k).astype(jnp.float32) * scale
        p = jax.nn.softmax(s, axis=-1)
        p = p.reshape(p.shape[0], A_HEADS, 2, p.shape[2], p.shape[3])
        diff = p[:, :, 0] - lam * p[:, :, 1]
        o = jnp.einsum('bhqk,bkhd->bqhd', diff.astype(v.dtype), v)
        return rms_norm(o, subln_g) * (1.0 - lambda_init)

    qc, kc, vc = project(h_ctx)
    ql, kl, vl = project(h_lat)
    cos, sin = axial_rope_tables(h_lat.shape[1], A_HEAD_DIM)
    ql, kl = apply_rope(ql, cos, sin), apply_rope(kl, cos, sin)
    k_all = jnp.concatenate([kc, kl], axis=1)
    v_all = jnp.concatenate([vc, vl], axis=1)
    ol = sweep_query_blocks(lambda i, qb: attend(qb, k_all, v_all), ql)
    oc = merge_heads(attend(qc, kc, vc), w_o) if need_ctx else None
    return oc, merge_heads(ol, w_o)


def mixer_window(h_ctx, h_lat, need_ctx, w_qkv, w_o, sink):
    nq, nkv = B_HEADS * B_HEAD_DIM, B_KV_HEADS * B_HEAD_DIM
    scale = B_HEAD_DIM ** -0.5

    def project(h):
        b, s, _ = h.shape
        q, k, v = jnp.split(h @ w_qkv, [nq, nq + nkv], axis=-1)
        return (q.reshape(b, s, B_HEADS, B_HEAD_DIM),
                k.reshape(b, s, B_KV_HEADS, B_HEAD_DIM),
                v.reshape(b, s, B_KV_HEADS, B_HEAD_DIM))

    qc, kc, vc = project(h_ctx)
    ql, kl, vl = project(h_lat)
    n_lat = h_lat.shape[1]
    cos, sin = axial_rope_tables(n_lat, B_HEAD_DIM)
    ql, kl = apply_rope(ql, cos, sin), apply_rope(kl, cos, sin)
    pad = ((0, 0), (B_WINDOW, B_WINDOW), (0, 0), (0, 0))
    kl_pad, vl_pad = jnp.pad(kl, pad), jnp.pad(vl, pad)
    span = Q_BLOCK + 2 * B_WINDOW
    ctx_mask = jnp.ones((Q_BLOCK, kc.shape[1]), dtype=bool)

    def block(i, qb):
        start = i * Q_BLOCK
        kw = lax.dynamic_slice_in_dim(kl_pad, start, span, axis=1)
        vw = lax.dynamic_slice_in_dim(vl_pad, start, span, axis=1)
        q_pos = start + jnp.arange(Q_BLOCK)
        k_pos = start - B_WINDOW + jnp.arange(span)
        band = (jnp.abs(q_pos[:, None] - k_pos[None, :]) <= B_WINDOW) & (k_pos >= 0) & (k_pos < n_lat)
        mask = jnp.concatenate([ctx_mask, band], axis=1)
        return gqa_attend(qb, jnp.concatenate([kc, kw], axis=1), jnp.concatenate([vc, vw], axis=1),
                          scale, mask=mask, sink=sink)

    ol = sweep_query_blocks(block, ql)
    oc = merge_heads(gqa_attend(qc, kc, vc, scale, sink=sink), w_o) if need_ctx else None
    return oc, merge_heads(ol, w_o)


def mixer_mla(h_ctx, h_lat, need_ctx, w_down, q_norm_g, kv_norm_g, w_uq, w_ukv, w_o):
    scale = (C_NOPE_DIM + C_ROPE_DIM) ** -0.5

    def project(h, rope):
        b, s, _ = h.shape
        cq, ckv, k_rope = jnp.split(h @ w_down, [C_Q_RANK, C_Q_RANK + C_KV_RANK], axis=-1)
        q = (rms_norm(cq, q_norm_g) @ w_uq).reshape(b, s, C_HEADS, C_NOPE_DIM + C_ROPE_DIM)
        kv = (rms_norm(ckv, kv_norm_g) @ w_ukv).reshape(b, s, C_HEADS, C_NOPE_DIM + C_V_DIM)
        q_nope, q_rope = jnp.split(q, [C_NOPE_DIM], axis=-1)
        k_nope, v = jnp.split(kv, [C_NOPE_DIM], axis=-1)
        k_rope = k_rope[:, :, None, :]
        if rope is not None:
            q_rope = apply_rope(q_rope, *rope)
            k_rope = apply_rope(k_rope, *rope)
        q = jnp.concatenate([q_nope, q_rope], axis=-1)
        k = jnp.concatenate([k_nope, jnp.broadcast_to(k_rope, (b, s, C_HEADS, C_ROPE_DIM))], axis=-1)
        return q, k, v

    qc, kc, vc = project(h_ctx, None)
    ql, kl, vl = project(h_lat, axial_rope_tables(h_lat.shape[1], C_ROPE_DIM))
    k_all = jnp.concatenate([kc, kl], axis=1)
    v_all = jnp.concatenate([vc, vl], axis=1)
    ol = sweep_query_blocks(lambda i, qb: gqa_attend(qb, k_all, v_all, scale), ql)
    oc = merge_heads(gqa_attend(qc, kc, vc, scale), w_o) if need_ctx else None
    return oc, merge_heads(ol, w_o)


def mixer_qknorm(h_ctx, h_lat, need_ctx, w_qkv, q_norm_g, k_norm_g, w_o):
    nq, nkv = D_HEADS * D_HEAD_DIM, D_KV_HEADS * D_HEAD_DIM
    scale = D_HEAD_DIM ** -0.5

    def project(h):
        b, s, _ = h.shape
        q, k, v = jnp.split(h @ w_qkv, [nq, nq + nkv], axis=-1)
        q = rms_norm(q.reshape(b, s, D_HEADS, D_HEAD_DIM), q_norm_g)
        k = rms_norm(k.reshape(b, s, D_KV_HEADS, D_HEAD_DIM), k_norm_g)
        return q, k, v.reshape(b, s, D_KV_HEADS, D_HEAD_DIM)

    qc, kc, vc = project(h_ctx)
    ql, kl, vl = project(h_lat)
    cos, sin = axial_rope_tables(h_lat.shape[1], D_HEAD_DIM)
    ql, kl = apply_rope(ql, cos, sin), apply_rope(kl, cos, sin)
    k_all = jnp.concatenate([kc, kl], axis=1)
    v_all = jnp.concatenate([vc, vl], axis=1)
    ol = sweep_query_blocks(lambda i, qb: gqa_attend(qb, k_all, v_all, scale), ql)
    oc = merge_heads(gqa_attend(qc, kc, vc, scale), w_o) if need_ctx else None
    return oc, merge_heads(ol, w_o)


def conv_ffn(h, w_up, conv_w, conv_b, w_down):
    s = h.shape[1]
    u = h @ w_up
    half = CONV_W // 2
    u_pad = jnp.pad(u, ((0, 0), (half, half), (0, 0)))
    y = conv_b
    for j in range(CONV_W):
        y = y + u_pad[:, j:j + s] * conv_w[j]
    a, g = jnp.split(y, 2, axis=-1)
    return (jax.nn.silu(a) * g) @ w_down


def setup_inputs(seed: int = 0) -> dict:
    key = jax.random.key(seed)
    ks = iter(jax.random.split(key, 40))

    def nrm(shape, scale):
        return jax.random.normal(next(ks), shape, jnp.float32) * scale

    def gain(shape):
        return 1.0 + nrm(shape, 0.02)

    D, L, R = D_MODEL, DEPTH, N_CYCLES
    a_width = 2 * A_HEADS * A_HEAD_DIM
    return {
        'x': nrm((BATCH, SEQ, D), 1.0),
        'c': nrm((BATCH, D), 1.0),
        'ctx': nrm((BATCH, CTX_LEN, D), 1.0),
        'c_ctx': nrm((D,), 1.0),
        'ada_w': nrm((L, D, 6 * D), 0.5 * D ** -0.5),
        'ada_b': nrm((L, 6 * D), 0.01),
        'norm1_g': gain((L, D)),
        'norm2_g': gain((L, D)),
        'ffn_up': nrm((L, D, 2 * D_FF), D ** -0.5),
        'ffn_conv_w': nrm((L, CONV_W, 2 * D_FF), CONV_W ** -0.5),
        'ffn_conv_b': nrm((L, 2 * D_FF), 0.01),
        'ffn_down': nrm((L, D_FF, D), D_FF ** -0.5),
        'a_w_qkv': nrm((R, D, 3 * a_width), D ** -0.5),
        'a_w_o': nrm((R, a_width, D), a_width ** -0.5),
        'a_lambda_q1': nrm((R, A_HEAD_DIM), 0.1),
        'a_lambda_k1': nrm((R, A_HEAD_DIM), 0.1),
        'a_lambda_q2': nrm((R, A_HEAD_DIM), 0.1),
        'a_lambda_k2': nrm((R, A_HEAD_DIM), 0.1),
        'a_subln_g': gain((R, 2 * A_HEAD_DIM)),
        'b_w_qkv': nrm((R, D, (B_HEADS + 2 * B_KV_HEADS) * B_HEAD_DIM), D ** -0.5),
        'b_w_o': nrm((R, B_HEADS * B_HEAD_DIM, D), (B_HEADS * B_HEAD_DIM) ** -0.5),
        'b_sink': nrm((R, B_HEADS), 0.5),
        'c_w_down': nrm((R, D, C_Q_RANK + C_KV_RANK + C_ROPE_DIM), D ** -0.5),
        'c_q_norm_g': gain((R, C_Q_RANK)),
        'c_kv_norm_g': gain((R, C_KV_RANK)),
        'c_w_uq': nrm((R, C_Q_RANK, C_HEADS * (C_NOPE_DIM + C_ROPE_DIM)), C_Q_RANK ** -0.5),
        'c_w_ukv': nrm((R, C_KV_RANK, C_HEADS * (C_NOPE_DIM + C_V_DIM)), C_KV_RANK ** -0.5),
        'c_w_o': nrm((R, C_HEADS * C_V_DIM, D), (C_HEADS * C_V_DIM) ** -0.5),
        'd_w_qkv': nrm((R, D, (D_HEADS + 2 * D_KV_HEADS) * D_HEAD_DIM), D ** -0.5),
        'd_q_norm_g': gain((R, D_HEAD_DIM)),
        'd_k_norm_g': gain((R, D_HEAD_DIM)),
        'd_w_o': nrm((R, D_HEADS * D_HEAD_DIM, D), (D_HEADS * D_HEAD_DIM) ** -0.5),
        'final_g': gain((D,)),
    }


def reference(x, c, ctx, c_ctx, ada_w, ada_b, norm1_g, norm2_g, ffn_up, ffn_conv_w, ffn_conv_b, ffn_down,
              a_w_qkv, a_w_o, a_lambda_q1, a_lambda_k1, a_lambda_q2, a_lambda_k2, a_subln_g,
              b_w_qkv, b_w_o, b_sink,
              c_w_down, c_q_norm_g, c_kv_norm_g, c_w_uq, c_w_ukv, c_w_o,
              d_w_qkv, d_q_norm_g, d_k_norm_g, d_w_o,
              final_g):
    silu_c = jax.nn.silu(c)
    silu_cc = jax.nn.silu(c_ctx)
    for i in range(DEPTH):
        kind, r = i % N_MIXERS, i // N_MIXERS
        need_ctx = i < DEPTH - 1
        mod_l = jnp.split((silu_c @ ada_w[i] + ada_b[i])[:, None, :], 6, axis=-1)
        mod_c = jnp.split(silu_cc @ ada_w[i] + ada_b[i], 6, axis=-1)
        hl = modulate(x, norm1_g[i], mod_l[0], mod_l[1])
        hc = modulate(ctx, norm1_g[i], mod_c[0], mod_c[1])
        if kind == 0:
            oc, ol = mixer_diff(hc, hl, i, need_ctx, a_w_qkv[r], a_w_o[r], a_lambda_q1[r], a_lambda_k1[r],
                                a_lambda_q2[r], a_lambda_k2[r], a_subln_g[r])
        elif kind == 1:
            oc, ol = mixer_window(hc, hl, need_ctx, b_w_qkv[r], b_w_o[r], b_sink[r])
        elif kind == 2:
            oc, ol = mixer_mla(hc, hl, need_ctx, c_w_down[r], c_q_norm_g[r], c_kv_norm_g[r], c_w_uq[r],
                               c_w_ukv[r], c_w_o[r])
        else:
            oc, ol = mixer_qknorm(hc, hl, need_ctx, d_w_qkv[r], d_q_norm_g[r], d_k_norm_g[r], d_w_o[r])
        x = x + mod_l[2] * ol
        x = x + mod_l[5] * conv_ffn(modulate(x, norm2_g[i], mod_l[3], mod_l[4]),
                                    ffn_up[i], ffn_conv_w[i], ffn_conv_b[i], ffn_down[i])
        if need_ctx:
            ctx = ctx + mod_c[2] * oc
            ctx = ctx + mod_c[5] * conv_ffn(modulate(ctx, norm2_g[i], mod_c[3], mod_c[4]),
                                            ffn_up[i], ffn_conv_w[i], ffn_conv_b[i], ffn_down[i])
    return rms_norm(x, final_g)
```

```python
import functools
import math

import jax
import jax.numpy as jnp
from jax import lax
from jax.experimental import pallas as pl
from jax.experimental.pallas import tpu as pltpu

F32 = jnp.float32
BF16 = jnp.bfloat16

GRID_W = 64
ROPE_THETA = 10000.0
NORM_EPS = 1e-6
NEG_INF = -1e30
LOG2E = math.log2(math.e)

A_HEADS, A_HEAD_DIM = 8, 64
B_HEADS, B_KV_HEADS, B_HEAD_DIM, B_WINDOW = 16, 4, 64, 128
C_HEADS, C_Q_RANK, C_KV_RANK, C_NOPE_DIM, C_ROPE_DIM, C_V_DIM = 8, 384, 256, 128, 64, 128
D_HEADS, D_KV_HEADS, D_HEAD_DIM = 8, 4, 128
CONV_W = 3

LANES = 128
SUBLANES = 8
MOD_ROWS = 16
VMEM_LIMIT = 56 * 1024 * 1024
FF_CHUNK = 256
HALO = SUBLANES


def _params(n_axes):
    return pltpu.CompilerParams(dimension_semantics=("arbitrary",) * n_axes,
                                vmem_limit_bytes=VMEM_LIMIT)


def _dot(a, b):
    return jnp.dot(a, b, preferred_element_type=F32)


def _dot_nt(a, b):
    return lax.dot_general(a, b, (((1,), (1,)), ((), ())), preferred_element_type=F32)


def _lane_iota(n=LANES):
    return lax.broadcasted_iota(jnp.int32, (1, n), 1)


def _ada_kernel(c_ref, w_ref, b_ref, o_ref):
    cv = c_ref[...]
    s = cv / (1.0 + jnp.exp(-cv))
    o_ref[0] = _dot(s.astype(BF16), w_ref[0].astype(BF16)) + b_ref[0]


def _ada_mods(c, c_ctx, ada_w, ada_b):
    n_layers, d, n_out = ada_w.shape
    batch = c.shape[0]
    rows = jnp.zeros((MOD_ROWS, d), F32).at[:batch].set(c).at[batch].set(c_ctx)
    tn = 1536
    out = pl.pallas_call(
        _ada_kernel,
        grid=(n_layers, n_out // tn),
        in_specs=[pl.BlockSpec((MOD_ROWS, d), lambda l, j: (0, 0)),
                  pl.BlockSpec((1, d, tn), lambda l, j: (l, 0, j)),
                  pl.BlockSpec((1, 1, tn), lambda l, j: (l, 0, j))],
        out_specs=pl.BlockSpec((1, MOD_ROWS, tn), lambda l, j: (l, 0, j)),
        out_shape=jax.ShapeDtypeStruct((n_layers, MOD_ROWS, n_out), F32),
        compiler_params=_params(2),
        name="ada_mods",
    )(rows, ada_w, ada_b.reshape(n_layers, 1, n_out))
    return out.reshape(n_layers, MOD_ROWS, 6, d)


def _modulated(x, mod, g, shift_row, scale_row):
    ms = jnp.mean(x * x, axis=-1, keepdims=True)
    y = x * lax.rsqrt(ms + NORM_EPS)
    return (y * g) * (1.0 + mod[scale_row:scale_row + 1]) + mod[shift_row:shift_row + 1]


def _rms(x, g):
    ms = jnp.mean(x * x, axis=-1, keepdims=True)
    return x * lax.rsqrt(ms + NORM_EPS) * g


def _rope64(xc, cos, sin, lo32):
    r_dn = pltpu.roll(xc, 32, 1)
    r_up = pltpu.roll(xc, 96, 1)
    return xc * cos + jnp.where(lo32, r_up, r_dn) * sin


def _rope128(xc, cos, sin):
    return xc * cos + pltpu.roll(xc, 64, 1) * sin


def _rope_tables(n_tokens, rot_dim):
    n_rows = n_tokens // GRID_W
    rows = jnp.repeat(jnp.arange(n_rows, dtype=F32), GRID_W)
    cols = jnp.tile(jnp.arange(GRID_W, dtype=F32), n_rows)
    n_freq = rot_dim // 4
    inv_freq = ROPE_THETA ** (-jnp.arange(n_freq, dtype=F32) / n_freq)
    ang = jnp.concatenate([rows[:, None] * inv_freq, cols[:, None] * inv_freq], axis=-1)
    cos, sin = jnp.cos(ang), jnp.sin(ang)
    reps = LANES // rot_dim
    return (jnp.tile(jnp.concatenate([cos, cos], axis=-1), (1, reps)),
            jnp.tile(jnp.concatenate([-sin, sin], axis=-1), (1, reps)))


def _softmax_pv(scores, values, sink=None):
    m = None
    for s in scores:
        mi = jnp.max(s, axis=-1, keepdims=True)
        m = mi if m is None else jnp.maximum(m, mi)
    if sink is not None:
        m = jnp.maximum(m, sink)
    l = None if sink is None else jnp.exp2(sink - m)
    o = None
    for s, v in zip(scores, values):
        p = jnp.exp2(s - m)
        li = jnp.sum(p, axis=-1, keepdims=True)
        l = li if l is None else l + li
        oi = _dot(p.astype(BF16), v)
        o = oi if o is None else o + oi
    return o / l


def _proj_a_kernel(rope, *refs):
    if rope:
        x_ref, mod_ref, g_ref, cos_ref, sin_ref, w_ref, q_ref, k_ref, v_ref = refs
    else:
        x_ref, mod_ref, g_ref, w_ref, q_ref, k_ref, v_ref = refs
    h = _modulated(x_ref[...], mod_ref[0], g_ref[...], 0, 1).astype(BF16)
    qkv = _dot(h, w_ref[...])
    width = q_ref.shape[1]
    q_scale = A_HEAD_DIM ** -0.5 * LOG2E
    lo32 = (_lane_iota() % 64) < 32
    for c in range(width // LANES):
        sl = slice(c * LANES, (c + 1) * LANES)
        qc = qkv[:, sl]
        kc = qkv[:, width + c * LANES: width + (c + 1) * LANES]
        if rope:
            qc = _rope64(qc, cos_ref[...], sin_ref[...], lo32)
            kc = _rope64(kc, cos_ref[...], sin_ref[...], lo32)
        q_ref[:, sl] = (qc * q_scale).astype(BF16)
        k_ref[:, sl] = kc.astype(BF16)
    v_ref[...] = qkv[:, 2 * width:].astype(BF16)


def _proj_b_kernel(rope, *refs):
    if rope:
        x_ref, mod_ref, g_ref, cos_ref, sin_ref, w_ref, q_ref, k_ref, v_ref = refs
    else:
        x_ref, mod_ref, g_ref, w_ref, q_ref, k_ref, v_ref = refs
    h = _modulated(x_ref[...], mod_ref[0], g_ref[...], 0, 1).astype(BF16)
    qkv = _dot(h, w_ref[...])
    nq, nkv = q_ref.shape[1], k_ref.shape[1]
    q_scale = B_HEAD_DIM ** -0.5 * LOG2E
    lo32 = (_lane_iota() % 64) < 32
    for c in range(nq // LANES):
        sl = slice(c * LANES, (c + 1) * LANES)
        qc = qkv[:, sl]
        if rope:
            qc = _rope64(qc, cos_ref[...], sin_ref[...], lo32)
        q_ref[:, sl] = (qc * q_scale).astype(BF16)
    for c in range(nkv // LANES):
        sl = slice(c * LANES, (c + 1) * LANES)
        kc = qkv[:, nq + c * LANES: nq + (c + 1) * LANES]
        if rope:
            kc = _rope64(kc, cos_ref[...], sin_ref[...], lo32)
        k_ref[:, sl] = kc.astype(BF16)
    v_ref[...] = qkv[:, nq + nkv:].astype(BF16)


def _proj_c_kernel(rope, *refs):
    if rope:
        (x_ref, mod_ref, g_ref, cos_ref, sin_ref, wd_ref, qg_ref, kvg_ref, wuq_ref, wukv_ref,
         q_ref, k_ref, v_ref) = refs
    else:
        x_ref, mod_ref, g_ref, wd_ref, qg_ref, kvg_ref, wuq_ref, wukv_ref, q_ref, k_ref, v_ref = refs
    h = _modulated(x_ref[...], mod_ref[0], g_ref[...], 0, 1).astype(BF16)
    down = _dot(h, wd_ref[...])
    cq = _rms(down[:, :C_Q_RANK], qg_ref[...]).astype(BF16)
    ckv = _rms(down[:, C_Q_RANK:C_Q_RANK + C_KV_RANK], kvg_ref[...]).astype(BF16)
    k_rope = down[:, C_Q_RANK + C_KV_RANK:]
    q = _dot(cq, wuq_ref[...])
    kv = _dot(ckv, wukv_ref[...])
    q_scale = (C_NOPE_DIM + C_ROPE_DIM) ** -0.5 * LOG2E
    lo32 = (_lane_iota() % 64) < 32
    if rope:
        k_rope = _rope64(k_rope, cos_ref[...], sin_ref[...], lo32)
    k_rope = k_rope.astype(BF16)
    for hd in range(C_HEADS):
        lo = slice(2 * hd * LANES, (2 * hd + 1) * LANES)
        hi = slice((2 * hd + 1) * LANES, (2 * hd + 2) * LANES)
        q_rope = q[:, hi]
        if rope:
            q_rope = _rope64(q_rope, cos_ref[...], sin_ref[...], lo32)
        q_ref[:, lo] = (q[:, lo] * q_scale).astype(BF16)
        q_ref[:, hi] = (q_rope * q_scale).astype(BF16)
        k_ref[:, lo] = kv[:, lo].astype(BF16)
        k_ref[:, hi] = k_rope
        v_ref[:, hd * LANES:(hd + 1) * LANES] = kv[:, hi].astype(BF16)


def _proj_d_kernel(rope, *refs):
    if rope:
        x_ref, mod_ref, g_ref, cos_ref, sin_ref, w_ref, qg_ref, kg_ref, q_ref, k_ref, v_ref = refs
    else:
        x_ref, mod_ref, g_ref, w_ref, qg_ref, kg_ref, q_ref, k_ref, v_ref = refs
    h = _modulated(x_ref[...], mod_ref[0], g_ref[...], 0, 1).astype(BF16)
    qkv = _dot(h, w_ref[...])
    nq, nkv = q_ref.shape[1], k_ref.shape[1]
    q_scale = D_HEAD_DIM ** -0.5 * LOG2E
    for c in range(nq // LANES):
        sl = slice(c * LANES, (c + 1) * LANES)
        qc = _rms(qkv[:, sl], qg_ref[...])
        if rope:
            qc = _rope128(qc, cos_ref[...], sin_ref[...])
        q_ref[:, sl] = (qc * q_scale).astype(BF16)
    for c in range(nkv // LANES):
        sl = slice(c * LANES, (c + 1) * LANES)
        kc = _rms(qkv[:, nq + c * LANES: nq + (c + 1) * LANES], kg_ref[...])
        if rope:
            kc = _rope128(kc, cos_ref[...], sin_ref[...])
        k_ref[:, sl] = kc.astype(BF16)
    v_ref[...] = qkv[:, nq + nkv:].astype(BF16)


def _project(body, x, mod, mod_row, g, rope_tabs, weights, out_widths, tm, seq):
    rows, d = x.shape
    tiles_per_seq = seq // tm
    const = lambda i: (0, 0)
    in_specs = [pl.BlockSpec((tm, d), lambda i: (i, 0)),
                pl.BlockSpec((1, 6, d), lambda i: (mod_row(i), 0, 0)),
                pl.BlockSpec((1, d), const)]
    args = [x, mod, g.reshape(1, d)]
    if rope_tabs is not None:
        in_specs += [pl.BlockSpec((tm, LANES), lambda i: (i % tiles_per_seq, 0))] * 2
        args += list(rope_tabs)
    for w in weights:
        in_specs.append(pl.BlockSpec(w.shape, const))
        args.append(w)
    return pl.pallas_call(
        functools.partial(body, rope_tabs is not None),
        grid=(rows // tm,),
        in_specs=in_specs,
        out_specs=[pl.BlockSpec((tm, w), lambda i: (i, 0)) for w in out_widths],
        out_shape=[jax.ShapeDtypeStruct((rows, w), BF16) for w in out_widths],
        compiler_params=_params(1),
        name=body.__name__.strip("_"),
    )(*args)


def _attn_a_kernel(n_seg, lambda_init, *refs):
    q_ref = refs[0]
    k_refs = refs[1:1 + n_seg]
    v_refs = refs[1 + n_seg:1 + 2 * n_seg]
    lq1_ref, lk1_ref, lq2_ref, lk2_ref, g_ref, o_ref = refs[1 + 2 * n_seg:]
    lam = (jnp.exp(jnp.sum(lq1_ref[...] * lk1_ref[...], keepdims=True))
           - jnp.exp(jnp.sum(lq2_ref[...] * lk2_ref[...], keepdims=True)) + lambda_init)
    q = q_ref[0]
    lane = _lane_iota()
    ks = [r[0] for r in k_refs]
    vs = [r[0] for r in v_refs]
    outs = []
    for half in range(2):
        qh = jnp.where((lane >= 64) if half else (lane < 64), q, jnp.zeros_like(q))
        outs.append(_softmax_pv([_dot_nt(qh, k) for k in ks], vs))
    diff = outs[0] - lam * outs[1]
    o_ref[0] = (_rms(diff, g_ref[...]) * (1.0 - lambda_init)).astype(BF16)


def _attn_b_kernel(n_seg, tq, n_lat, *refs):
    q_ref = refs[0]
    k_refs = refs[1:1 + n_seg]
    v_refs = refs[1 + n_seg:1 + 2 * n_seg]
    sink_ref, o_ref = refs[1 + 2 * n_seg:]
    pair = pl.program_id(1)
    lane = _lane_iota()
    ks = [k_refs[0][0]]
    vs = [v_refs[0][0]]
    band = None
    if n_seg == 2:
        span = tq + 2 * B_WINDOW
        start = pl.program_id(2) * tq
        k0 = pl.multiple_of(jnp.clip(start - B_WINDOW, 0, n_lat - span), LANES)
        ks.append(k_refs[1][0, pl.ds(k0, span), :])
        vs.append(v_refs[1][0, pl.ds(k0, span), :])
        q_pos = start + lax.broadcasted_iota(jnp.int32, (tq, 1), 0)
        k_pos = k0 + lax.broadcasted_iota(jnp.int32, (1, span), 1)
        band = jnp.abs(q_pos - k_pos) <= B_WINDOW
    group = B_HEADS // B_KV_HEADS
    for c in range(group):
        q = q_ref[0, :, c * LANES:(c + 1) * LANES]
        outs = []
        for half in range(2):
            qh = jnp.where((lane >= 64) if half else (lane < 64), q, jnp.zeros_like(q))
            scores = [_dot_nt(qh, k) for k in ks]
            if band is not None:
                scores[1] = jnp.where(band, scores[1], NEG_INF)
            head = (2 * pair + half) * group + c
            sink = jnp.full((1, 1), sink_ref[head], F32) * LOG2E
            outs.append(_softmax_pv(scores, vs, sink=sink))
        o_ref[0, :, c * LANES:(c + 1) * LANES] = jnp.where(lane < 64, outs[0], outs[1]).astype(BF16)


def _attn_plain_kernel(n_seg, n_heads, dq, dv, *refs):
    q_ref = refs[0]
    k_refs = refs[1:1 + n_seg]
    v_refs = refs[1 + n_seg:1 + 2 * n_seg]
    o_ref = refs[1 + 2 * n_seg]
    ks = [r[0] for r in k_refs]
    vs = [r[0] for r in v_refs]
    for hd in range(n_heads):
        q = q_ref[0, :, hd * dq:(hd + 1) * dq]
        o = _softmax_pv([_dot_nt(q, k) for k in ks], vs)
        o_ref[0, :, hd * dv:(hd + 1) * dv] = o.astype(BF16)


def _attention(body, q, ks, vs, extra, extra_specs, n_groups, wq, wk, wv, wo, tq, name):
    b, s, _ = q.shape
    in_specs = [pl.BlockSpec((1, tq, wq), lambda bi, gi, qi: (bi, qi, gi))]
    for k in ks:
        in_specs.append(pl.BlockSpec((1, k.shape[1], wk), lambda bi, gi, qi: (bi, 0, gi)))
    for v in vs:
        in_specs.append(pl.BlockSpec((1, v.shape[1], wv), lambda bi, gi, qi: (bi, 0, gi)))
    return pl.pallas_call(
        body,
        grid=(b, n_groups, s // tq),
        in_specs=in_specs + extra_specs,
        out_specs=pl.BlockSpec((1, tq, wo), lambda bi, gi, qi: (bi, qi, gi)),
        out_shape=jax.ShapeDtypeStruct((b, s, n_groups * wo), BF16),
        compiler_params=_params(3),
        name=name,
    )(q, *ks, *vs, *extra)


def _small_spec(a):
    return pl.BlockSpec(a.shape, lambda bi, gi, qi: (0,) * a.ndim)


def _wo_kernel(o_ref, x_ref, mod_ref, w_ref, out_ref):
    out_ref[...] = x_ref[...] + mod_ref[0, 2:3] * _dot(o_ref[...], w_ref[...])


def _out_proj(o, x, mod, mod_row, w_o, tm):
    rows, d = x.shape
    return pl.pallas_call(
        _wo_kernel,
        grid=(rows // tm,),
        in_specs=[pl.BlockSpec((tm, o.shape[1]), lambda i: (i, 0)),
                  pl.BlockSpec((tm, d), lambda i: (i, 0)),
                  pl.BlockSpec((1, 6, d), lambda i: (mod_row(i), 0, 0)),
                  pl.BlockSpec(w_o.shape, lambda i: (0, 0))],
        out_specs=pl.BlockSpec((tm, d), lambda i: (i, 0)),
        out_shape=jax.ShapeDtypeStruct((rows, d), F32),
        compiler_params=_params(1),
        name="out_proj",
    )(o, x, mod, w_o)


def _ffn_kernel(tiles_per_seq, final, *refs):
    if final:
        (x_ref, xp_ref, xn_ref, mod_ref, g_ref, wup_ref, cw_ref, cb_ref, wdn_ref, fg_ref,
         out_ref) = refs
    else:
        x_ref, xp_ref, xn_ref, mod_ref, g_ref, wup_ref, cw_ref, cb_ref, wdn_ref, out_ref = refs
    tm, d = x_ref.shape
    pos = pl.program_id(0) % tiles_per_seq
    mod = mod_ref[0]
    g = g_ref[...]
    x = x_ref[...]
    keep_prev = (pos > 0).astype(F32)
    keep_next = (pos < tiles_per_seq - 1).astype(F32)
    h_ext = jnp.concatenate([_modulated(xp_ref[...], mod, g, 3, 4) * keep_prev,
                             _modulated(x, mod, g, 3, 4),
                             _modulated(xn_ref[...], mod, g, 3, 4) * keep_next],
                            axis=0).astype(BF16)
    n_ext = tm + 2 * HALO
    acc = jnp.zeros((tm, d), F32)
    for c in range(wup_ref.shape[0]):
        u = _dot(h_ext, wup_ref[c])
        u_prev = pltpu.roll(u, 1, 0)[HALO:HALO + tm]
        u_next = pltpu.roll(u, n_ext - 1, 0)[HALO:HALO + tm]
        cw = cw_ref[c]
        y = cb_ref[c] + u_prev * cw[0:1] + u[HALO:HALO + tm] * cw[1:2] + u_next * cw[2:3]
        a, gate = y[:, :FF_CHUNK], y[:, FF_CHUNK:]
        act = (a / (1.0 + jnp.exp(-a))) * gate
        acc = acc + _dot(act.astype(BF16), wdn_ref[c])
    out = x + mod[5:6] * acc
    if final:
        out = _rms(out, fg_ref[...])
    out_ref[...] = out


def _conv_ffn(x, mod, mod_row, g, w_up, conv_w, conv_b, w_down, tm, seq, final_g=None):
    rows, d = x.shape
    tiles_per_seq = seq // tm
    halo_blocks = tm // HALO
    last_block = rows // HALO - 1
    const2 = lambda i: (0, 0)
    const3 = lambda i: (0, 0, 0)
    in_specs = [pl.BlockSpec((tm, d), lambda i: (i, 0)),
                pl.BlockSpec((HALO, d), lambda i: (jnp.maximum(i * halo_blocks - 1, 0), 0)),
                pl.BlockSpec((HALO, d), lambda i: (jnp.minimum((i + 1) * halo_blocks, last_block), 0)),
                pl.BlockSpec((1, 6, d), lambda i: (mod_row(i), 0, 0)),
                pl.BlockSpec((1, d), const2),
                pl.BlockSpec(w_up.shape, const3),
                pl.BlockSpec(conv_w.shape, const3),
                pl.BlockSpec(conv_b.shape, const3),
                pl.BlockSpec(w_down.shape, const3)]
    args = [x, x, x, mod, g.reshape(1, d), w_up, conv_w, conv_b, w_down]
    if final_g is not None:
        in_specs.append(pl.BlockSpec((1, d), const2))
        args.append(final_g.reshape(1, d))
    return pl.pallas_call(
        functools.partial(_ffn_kernel, tiles_per_seq, final_g is not None),
        grid=(rows // tm,),
        in_specs=in_specs,
        out_specs=pl.BlockSpec((tm, d), lambda i: (i, 0)),
        out_shape=jax.ShapeDtypeStruct((rows, d), F32),
        compiler_params=_params(1),
        name="conv_ffn",
    )(*args)


def _ffn_weights(w_up, conv_w, conv_b, w_down):
    d, two_ff = w_up.shape
    ff = two_ff // 2
    n = ff // FF_CHUNK

    def regroup(t):
        a = t[..., :ff].reshape(t.shape[:-1] + (n, FF_CHUNK))
        g = t[..., ff:].reshape(t.shape[:-1] + (n, FF_CHUNK))
        return jnp.moveaxis(jnp.concatenate([a, g], axis=-1), -2, 0)

    return (regroup(w_up).astype(BF16), regroup(conv_w), regroup(conv_b.reshape(1, two_ff)),
            w_down.reshape(n, FF_CHUNK, d).astype(BF16))


def kernel(x, c, ctx, c_ctx, ada_w, ada_b, norm1_g, norm2_g, ffn_up, ffn_conv_w, ffn_conv_b, ffn_down,
           a_w_qkv, a_w_o, a_lambda_q1, a_lambda_k1, a_lambda_q2, a_lambda_k2, a_subln_g,
           b_w_qkv, b_w_o, b_sink,
           c_w_down, c_q_norm_g, c_kv_norm_g, c_w_uq, c_w_ukv, c_w_o,
           d_w_qkv, d_q_norm_g, d_k_norm_g, d_w_o,
           final_g):
    batch, seq, d = x.shape
    n_ctx = ctx.shape[1]
    depth = ada_w.shape[0]
    n_mixers = 4

    mods = _ada_mods(c, c_ctx, ada_w, ada_b)
    rope64 = _rope_tables(seq, 64)
    rope128 = _rope_tables(seq, 128)

    tm_lat = min(512, seq)
    tm_ctx = n_ctx
    tq_lat = min(256, seq)
    tq_ctx = n_ctx
    lat_row = lambda i: i // (seq // tm_lat)
    ctx_row = lambda i: batch

    xl = x.reshape(batch * seq, d)
    xc = ctx.reshape(batch * n_ctx, d)

    def to3(t, n):
        return t.reshape(batch, n, t.shape[-1])

    for i in range(depth):
        kind, r = i % n_mixers, i // n_mixers
        need_ctx = i < depth - 1
        mod = mods[i]
        g1 = norm1_g[i]

        if kind == 0:
            width = 2 * A_HEADS * A_HEAD_DIM
            weights = [a_w_qkv[r].astype(BF16)]
            widths = [width, width, width]
            body, tabs = _proj_a_kernel, rope64
        elif kind == 1:
            nq, nkv = B_HEADS * B_HEAD_DIM, B_KV_HEADS * B_HEAD_DIM
            group = B_HEADS // B_KV_HEADS
            order = [(2 * p + half) * group + g for p in range(B_KV_HEADS // 2)
                     for g in range(group) for half in range(2)]
            perm = jnp.asarray([hq * B_HEAD_DIM + j for hq in order for j in range(B_HEAD_DIM)])
            w = b_w_qkv[r]
            weights = [jnp.concatenate([w[:, :nq][:, perm], w[:, nq:]], axis=1).astype(BF16)]
            widths = [nq, nkv, nkv]
            body, tabs = _proj_b_kernel, rope64
        elif kind == 2:
            wd = jnp.pad(c_w_down[r], ((0, 0), (0, LANES - C_ROPE_DIM)))
            wuq = c_w_uq[r].reshape(C_Q_RANK, C_HEADS, C_NOPE_DIM + C_ROPE_DIM)
            wuq = jnp.pad(wuq, ((0, 0), (0, 0), (0, 2 * LANES - C_NOPE_DIM - C_ROPE_DIM)))
            weights = [wd.astype(BF16), c_q_norm_g[r].reshape(1, -1), c_kv_norm_g[r].reshape(1, -1),
                       wuq.reshape(C_Q_RANK, C_HEADS * 2 * LANES).astype(BF16), c_w_ukv[r].astype(BF16)]
            widths = [C_HEADS * 2 * LANES, C_HEADS * 2 * LANES, C_HEADS * C_V_DIM]
            body, tabs = _proj_c_kernel, rope64
        else:
            weights = [d_w_qkv[r].astype(BF16), d_q_norm_g[r].reshape(1, -1), d_k_norm_g[r].reshape(1, -1)]
            widths = [D_HEADS * D_HEAD_DIM, D_KV_HEADS * D_HEAD_DIM, D_KV_HEADS * D_HEAD_DIM]
            body, tabs = _proj_d_kernel, rope128

        ql, kl, vl = _project(body, xl, mod, lat_row, g1, tabs, weights, widths, tm_lat, seq)
        qc, kc, vc = _project(body, xc, mod, ctx_row, g1, None, weights, widths, tm_ctx, n_ctx)
        ql, kl, vl = to3(ql, seq), to3(kl, seq), to3(vl, seq)
        qc, kc, vc = to3(qc, n_ctx), to3(kc, n_ctx), to3(vc, n_ctx)

        def attend(q, ks, vs, tq):
            n_seg = len(ks)
            if kind == 0:
                lambda_init = 0.8 - 0.6 * math.exp(-0.3 * i)
                extra = [a_lambda_q1[r].reshape(1, -1), a_lambda_k1[r].reshape(1, -1),
                         a_lambda_q2[r].reshape(1, -1), a_lambda_k2[r].reshape(1, -1),
                         a_subln_g[r].reshape(1, -1)]
                return _attention(functools.partial(_attn_a_kernel, n_seg, lambda_init), q, ks, vs, extra,
                                  [_small_spec(e) for e in extra], A_HEADS, LANES, LANES, LANES, LANES,
                                  tq, "attn_a")
            if kind == 1:
                group = B_HEADS // B_KV_HEADS
                return _attention(functools.partial(_attn_b_kernel, n_seg, tq, seq), q, ks, vs, [b_sink[r]],
                                  [pl.BlockSpec(memory_space=pltpu.SMEM)], B_KV_HEADS // 2,
                                  group * LANES, LANES, LANES, group * LANES, tq, "attn_b")
            if kind == 2:
                return _attention(functools.partial(_attn_plain_kernel, n_seg, 1, 2 * LANES, C_V_DIM),
                                  q, ks, vs, [], [], C_HEADS, 2 * LANES, 2 * LANES, C_V_DIM, C_V_DIM,
                                  tq, "attn_c")
            group = D_HEADS // D_KV_HEADS
            return _attention(functools.partial(_attn_plain_kernel, n_seg, group, D_HEAD_DIM, D_HEAD_DIM),
                              q, ks, vs, [], [], D_KV_HEADS, group * D_HEAD_DIM, D_HEAD_DIM, D_HEAD_DIM,
                              group * D_HEAD_DIM, tq, "attn_d")

        if kind == 0:
            w_o = a_w_o[r]
        elif kind == 1:
            w_o = b_w_o[r].reshape(B_HEADS, B_HEAD_DIM, d)[jnp.asarray(order)].reshape(-1, d)
        elif kind == 2:
            w_o = c_w_o[r]
        else:
            w_o = d_w_o[r]
        w_o = w_o.astype(BF16)
        ffn_w = _ffn_weights(ffn_up[i], ffn_conv_w[i], ffn_conv_b[i], ffn_down[i])

        ol = attend(ql, [kc, kl], [vc, vl], tq_lat).reshape(batch * seq, -1)
        xl = _out_proj(ol, xl, mod, lat_row, w_o, tm_lat)
        xl = _conv_ffn(xl, mod, lat_row, norm2_g[i], *ffn_w, tm_lat, seq,
                       final_g=None if need_ctx else final_g)
        if need_ctx:
            oc = attend(qc, [kc], [vc], tq_ctx).reshape(batch * n_ctx, -1)
            xc = _out_proj(oc, xc, mod, ctx_row, w_o, tm_ctx)
            xc = _conv_ffn(xc, mod, ctx_row, norm2_g[i], *ffn_w, tm_ctx, n_ctx)

    return xl.reshape(batch, seq, d)
```

```python
import functools
import math

import jax
import jax.numpy as jnp
from jax import lax
from jax.experimental import pallas as pl
from jax.experimental.pallas import tpu as pltpu

F32 = jnp.float32
BF16 = jnp.bfloat16

GRID_W = 64
ROPE_THETA = 10000.0
NORM_EPS = 1e-6
LOG2E = math.log2(math.e)

A_HEADS, A_HEAD_DIM = 8, 64
B_HEADS, B_KV_HEADS, B_HEAD_DIM, B_WINDOW = 16, 4, 64, 128
C_HEADS, C_Q_RANK, C_KV_RANK, C_NOPE_DIM, C_ROPE_DIM, C_V_DIM = 8, 384, 256, 128, 64, 128
D_HEADS, D_KV_HEADS, D_HEAD_DIM = 8, 4, 128
CONV_W = 3

LANES = 128
SUBLANES = 8
MOD_ROWS = 16
VMEM_LIMIT = 56 * 1024 * 1024
FF_CHUNK = 256
HALO = SUBLANES
O_HALO = 2 * SUBLANES
KEY_SEG = 512
STACK_B = 4


def _params(n_axes):
    return pltpu.CompilerParams(dimension_semantics=("arbitrary",) * n_axes,
                                vmem_limit_bytes=VMEM_LIMIT)


def _dot(a, b):
    return jnp.dot(a, b, preferred_element_type=F32)


def _dot_nt(a, b):
    return lax.dot_general(a, b, (((1,), (1,)), ((), ())), preferred_element_type=F32)


def _lane_iota(n=LANES):
    return lax.broadcasted_iota(jnp.int32, (1, n), 1)


def _ada_kernel(c_ref, w_ref, b_ref, o_ref):
    cv = c_ref[...]
    s = cv / (1.0 + jnp.exp(-cv))
    o_ref[0] = _dot(s.astype(BF16), w_ref[0].astype(BF16)) + b_ref[0]


def _ada_mods(c, c_ctx, ada_w, ada_b):
    n_layers, d, n_out = ada_w.shape
    batch = c.shape[0]
    rows = jnp.zeros((MOD_ROWS, d), F32).at[:batch].set(c).at[batch].set(c_ctx)
    tn = 1536
    out = pl.pallas_call(
        _ada_kernel,
        grid=(n_layers, n_out // tn),
        in_specs=[pl.BlockSpec((MOD_ROWS, d), lambda l, j: (0, 0)),
                  pl.BlockSpec((1, d, tn), lambda l, j: (l, 0, j)),
                  pl.BlockSpec((1, 1, tn), lambda l, j: (l, 0, j))],
        out_specs=pl.BlockSpec((1, MOD_ROWS, tn), lambda l, j: (l, 0, j)),
        out_shape=jax.ShapeDtypeStruct((n_layers, MOD_ROWS, n_out), F32),
        compiler_params=_params(2),
        name="ada_mods",
    )(rows, ada_w, ada_b.reshape(n_layers, 1, n_out))
    return out.reshape(n_layers, MOD_ROWS, 6, d)


def _modulated(x, mod, g, shift_row, scale_row):
    ms = jnp.mean(x * x, axis=-1, keepdims=True)
    y = x * lax.rsqrt(ms + NORM_EPS)
    return (y * g) * (1.0 + mod[scale_row:scale_row + 1]) + mod[shift_row:shift_row + 1]


def _rms(x, g):
    ms = jnp.mean(x * x, axis=-1, keepdims=True)
    return x * lax.rsqrt(ms + NORM_EPS) * g


def _rope64(xc, cos, sin, lo32):
    r_dn = pltpu.roll(xc, 32, 1)
    r_up = pltpu.roll(xc, 96, 1)
    return xc * cos + jnp.where(lo32, r_up, r_dn) * sin


def _rope128(xc, cos, sin):
    return xc * cos + pltpu.roll(xc, 64, 1) * sin


def _rope_tables(n_tokens, rot_dim):
    n_rows = n_tokens // GRID_W
    rows = jnp.repeat(jnp.arange(n_rows, dtype=F32), GRID_W)
    cols = jnp.tile(jnp.arange(GRID_W, dtype=F32), n_rows)
    n_freq = rot_dim // 4
    inv_freq = ROPE_THETA ** (-jnp.arange(n_freq, dtype=F32) / n_freq)
    ang = jnp.concatenate([rows[:, None] * inv_freq, cols[:, None] * inv_freq], axis=-1)
    cos, sin = jnp.cos(ang), jnp.sin(ang)
    reps = LANES // rot_dim
    return (jnp.tile(jnp.concatenate([cos, cos], axis=-1), (1, reps)),
            jnp.tile(jnp.concatenate([-sin, sin], axis=-1), (1, reps)))


def _lane_blocks(parts):
    return [s[:, j * LANES:(j + 1) * LANES] for s in parts for j in range(s.shape[1] // LANES)]


def _softmax_pv(scores, values, sink=None):
    m = jnp.max(functools.reduce(jnp.maximum, _lane_blocks(scores)), axis=-1, keepdims=True)
    if sink is not None:
        m = jnp.maximum(m, sink)
    o = functools.reduce(jnp.add, [_dot(jnp.exp2(s - m).astype(BF16), v) for s, v in zip(scores, values)])
    l = o[:, LANES:]
    if sink is not None:
        l = l + jnp.exp2(sink - m)
    return o[:, :LANES] / l


def _add_row_tiled(s, bias):
    r = bias.shape[0]
    return jnp.concatenate([s[j * r:(j + 1) * r] + bias for j in range(s.shape[0] // r)], axis=0)


def _attend_units(queries, keys, values, biases=None, sinks=None):
    assert all(v.shape[1] == LANES for v in values)
    values = [jnp.concatenate([v, jnp.ones_like(v)], axis=1) for v in values]
    all_scores = []
    for q in queries:
        scores = [_dot_nt(q, k) for k in keys]
        if biases is not None:
            scores = [s if b is None else _add_row_tiled(s, b) for s, b in zip(scores, biases)]
        all_scores.append(scores)
    sinks = sinks or [None] * len(queries)
    return [_softmax_pv(scores, values, sink) for scores, sink in zip(all_scores, sinks)]


def _segments(refs, seg):
    out = []
    for r in refs:
        n = r.shape[1]
        if n <= seg:
            out.append(r[0])
        else:
            out += [r[0, j * seg:(j + 1) * seg, :] for j in range(n // seg)]
    return out


def _proj_a_kernel(rope, *refs):
    if rope:
        x_ref, mod_ref, g_ref, cos_ref, sin_ref, w_ref, q_ref, k_ref, v_ref = refs
    else:
        x_ref, mod_ref, g_ref, w_ref, q_ref, k_ref, v_ref = refs
    h = _modulated(x_ref[...], mod_ref[0], g_ref[...], 0, 1).astype(BF16)
    qkv = _dot(h, w_ref[...])
    width = q_ref.shape[1]
    q_scale = A_HEAD_DIM ** -0.5 * LOG2E
    lo32 = (_lane_iota() % 64) < 32
    for c in range(width // LANES):
        sl = slice(c * LANES, (c + 1) * LANES)
        qc = qkv[:, sl]
        kc = qkv[:, width + c * LANES: width + (c + 1) * LANES]
        if rope:
            qc = _rope64(qc, cos_ref[...], sin_ref[...], lo32)
            kc = _rope64(kc, cos_ref[...], sin_ref[...], lo32)
        q_ref[:, sl] = (qc * q_scale).astype(BF16)
        k_ref[:, sl] = kc.astype(BF16)
    v_ref[...] = qkv[:, 2 * width:].astype(BF16)


def _proj_b_kernel(rope, *refs):
    if rope:
        x_ref, mod_ref, g_ref, cos_ref, sin_ref, w_ref, q_ref, k_ref, v_ref = refs
    else:
        x_ref, mod_ref, g_ref, w_ref, q_ref, k_ref, v_ref = refs
    h = _modulated(x_ref[...], mod_ref[0], g_ref[...], 0, 1).astype(BF16)
    qkv = _dot(h, w_ref[...])
    nq, nkv = q_ref.shape[1], k_ref.shape[1]
    q_scale = B_HEAD_DIM ** -0.5 * LOG2E
    lo32 = (_lane_iota() % 64) < 32
    for c in range(nq // LANES):
        sl = slice(c * LANES, (c + 1) * LANES)
        qc = qkv[:, sl]
        if rope:
            qc = _rope64(qc, cos_ref[...], sin_ref[...], lo32)
        q_ref[:, sl] = (qc * q_scale).astype(BF16)
    for c in range(nkv // LANES):
        sl = slice(c * LANES, (c + 1) * LANES)
        kc = qkv[:, nq + c * LANES: nq + (c + 1) * LANES]
        if rope:
            kc = _rope64(kc, cos_ref[...], sin_ref[...], lo32)
        k_ref[:, sl] = kc.astype(BF16)
    v_ref[...] = qkv[:, nq + nkv:].astype(BF16)


def _proj_c_kernel(rope, *refs):
    if rope:
        (x_ref, mod_ref, g_ref, cos_ref, sin_ref, wd_ref, qg_ref, kvg_ref, wuq_ref, wukv_ref,
         q_ref, k_ref, v_ref) = refs
    else:
        x_ref, mod_ref, g_ref, wd_ref, qg_ref, kvg_ref, wuq_ref, wukv_ref, q_ref, k_ref, v_ref = refs
    h = _modulated(x_ref[...], mod_ref[0], g_ref[...], 0, 1).astype(BF16)
    down = _dot(h, wd_ref[...])
    cq = _rms(down[:, :C_Q_RANK], qg_ref[...]).astype(BF16)
    ckv = _rms(down[:, C_Q_RANK:C_Q_RANK + C_KV_RANK], kvg_ref[...]).astype(BF16)
    k_rope = down[:, C_Q_RANK + C_KV_RANK:]
    q = _dot(cq, wuq_ref[...])
    kv = _dot(ckv, wukv_ref[...])
    q_scale = (C_NOPE_DIM + C_ROPE_DIM) ** -0.5 * LOG2E
    lo32 = (_lane_iota() % 64) < 32
    if rope:
        k_rope = _rope64(k_rope, cos_ref[...], sin_ref[...], lo32)
    k_rope = k_rope.astype(BF16)
    for hd in range(C_HEADS):
        lo = slice(2 * hd * LANES, (2 * hd + 1) * LANES)
        hi = slice((2 * hd + 1) * LANES, (2 * hd + 2) * LANES)
        q_rope = q[:, hi]
        if rope:
            q_rope = _rope64(q_rope, cos_ref[...], sin_ref[...], lo32)
        q_ref[:, lo] = (q[:, lo] * q_scale).astype(BF16)
        q_ref[:, hi] = (q_rope * q_scale).astype(BF16)
        k_ref[:, lo] = kv[:, lo].astype(BF16)
        k_ref[:, hi] = k_rope
        v_ref[:, hd * LANES:(hd + 1) * LANES] = kv[:, hi].astype(BF16)


def _proj_d_kernel(rope, *refs):
    if rope:
        x_ref, mod_ref, g_ref, cos_ref, sin_ref, w_ref, qg_ref, kg_ref, q_ref, k_ref, v_ref = refs
    else:
        x_ref, mod_ref, g_ref, w_ref, qg_ref, kg_ref, q_ref, k_ref, v_ref = refs
    h = _modulated(x_ref[...], mod_ref[0], g_ref[...], 0, 1).astype(BF16)
    qkv = _dot(h, w_ref[...])
    nq, nkv = q_ref.shape[1], k_ref.shape[1]
    q_scale = D_HEAD_DIM ** -0.5 * LOG2E
    for c in range(nq // LANES):
        sl = slice(c * LANES, (c + 1) * LANES)
        qc = _rms(qkv[:, sl], qg_ref[...])
        if rope:
            qc = _rope128(qc, cos_ref[...], sin_ref[...])
        q_ref[:, sl] = (qc * q_scale).astype(BF16)
    for c in range(nkv // LANES):
        sl = slice(c * LANES, (c + 1) * LANES)
        kc = _rms(qkv[:, nq + c * LANES: nq + (c + 1) * LANES], kg_ref[...])
        if rope:
            kc = _rope128(kc, cos_ref[...], sin_ref[...])
        k_ref[:, sl] = kc.astype(BF16)
    v_ref[...] = qkv[:, nq + nkv:].astype(BF16)


def _project(body, x, mod, mod_row, g, rope_tabs, weights, out_widths, tm, seq):
    rows, d = x.shape
    tiles_per_seq = seq // tm
    const = lambda i: (0, 0)
    in_specs = [pl.BlockSpec((tm, d), lambda i: (i, 0)),
                pl.BlockSpec((1, 6, d), lambda i: (mod_row(i), 0, 0)),
                pl.BlockSpec((1, d), const)]
    args = [x, mod, g.reshape(1, d)]
    if rope_tabs is not None:
        in_specs += [pl.BlockSpec((tm, LANES), lambda i: (i % tiles_per_seq, 0))] * 2
        args += list(rope_tabs)
    for w in weights:
        in_specs.append(pl.BlockSpec(w.shape, const))
        args.append(w)
    return pl.pallas_call(
        functools.partial(body, rope_tabs is not None),
        grid=(rows // tm,),
        in_specs=in_specs,
        out_specs=[pl.BlockSpec((tm, w), lambda i: (i, 0)) for w in out_widths],
        out_shape=[jax.ShapeDtypeStruct((rows, w), BF16) for w in out_widths],
        compiler_params=_params(1),
        name=body.__name__.strip("_"),
    )(*args)


def _attn_a_kernel(n_seg, n_units, lambda_init, *refs):
    q_ref = refs[0]
    k_refs = refs[1:1 + n_seg]
    v_refs = refs[1 + n_seg:1 + 2 * n_seg]
    lq1_ref, lk1_ref, lq2_ref, lk2_ref, g_ref, o_ref = refs[1 + 2 * n_seg:]
    lam = (jnp.exp(jnp.sum(lq1_ref[...] * lk1_ref[...], keepdims=True))
           - jnp.exp(jnp.sum(lq2_ref[...] * lk2_ref[...], keepdims=True)) + lambda_init)
    tq = q_ref.shape[1] // n_units
    lane = _lane_iota()
    zero = jnp.zeros((tq, LANES), BF16)
    queries = []
    for u in range(n_units):
        q = q_ref[0, u * tq:(u + 1) * tq, :]
        queries.append(jnp.concatenate([jnp.where(lane < 64, q, zero), jnp.where(lane >= 64, q, zero)],
                                       axis=0))
    outs = _attend_units(queries, _segments(k_refs, KEY_SEG), _segments(v_refs, KEY_SEG))
    for u, o in enumerate(outs):
        diff = o[:tq] - lam * o[tq:]
        o_ref[0, u * tq:(u + 1) * tq, :] = (_rms(diff, g_ref[...]) * (1.0 - lambda_init)).astype(BF16)


def _attn_b_kernel(n_seg, tq, n_lat, *refs):
    q_ref = refs[0]
    k_refs = refs[1:1 + n_seg]
    v_refs = refs[1 + n_seg:1 + 2 * n_seg]
    sink_ref, o_ref = refs[1 + 2 * n_seg:]
    pair = pl.program_id(1)
    lane = _lane_iota()
    group = B_HEADS // B_KV_HEADS
    ks = [k_refs[0][0]]
    vs = [v_refs[0][0]]
    masks = None
    if n_seg == 2:
        span = tq + 2 * B_WINDOW
        seg = span // 2
        start = pl.program_id(2) * tq
        k0 = pl.multiple_of(jnp.clip(start - B_WINDOW, 0, n_lat - span), LANES)
        q_pos = start + lax.broadcasted_iota(jnp.int32, (tq, 1), 0)
        masks = [None]
        for j in range(2):
            ks.append(k_refs[1][0, pl.ds(k0 + j * seg, seg), :])
            vs.append(v_refs[1][0, pl.ds(k0 + j * seg, seg), :])
            k_pos = k0 + j * seg + lax.broadcasted_iota(jnp.int32, (1, seg), 1)
            masks.append(jnp.where(jnp.abs(q_pos - k_pos) <= B_WINDOW, 0.0, -jnp.inf))
    queries, sinks = [], []
    for half in range(2):
        keep = (lane >= 64) if half else (lane < 64)
        for c0 in range(0, group, STACK_B):
            chunks = range(c0, c0 + STACK_B)
            queries.append(jnp.concatenate(
                [jnp.where(keep, q_ref[0, :, c * LANES:(c + 1) * LANES], jnp.zeros((tq, LANES), BF16))
                 for c in chunks], axis=0))
            sinks.append(jnp.concatenate(
                [jnp.full((tq, 1), sink_ref[(2 * pair + half) * group + c], F32) for c in chunks],
                axis=0) * LOG2E)
    outs = _attend_units(queries, ks, vs, masks, sinks)
    per_half = group // STACK_B
    for c in range(group):
        rows = slice((c % STACK_B) * tq, (c % STACK_B + 1) * tq)
        lo, hi = outs[c // STACK_B][rows], outs[per_half + c // STACK_B][rows]
        o_ref[0, :, c * LANES:(c + 1) * LANES] = jnp.where(lane < 64, lo, hi).astype(BF16)


def _attn_plain_kernel(n_seg, n_units, n_heads, dq, dv, *refs):
    q_ref = refs[0]
    k_refs = refs[1:1 + n_seg]
    v_refs = refs[1 + n_seg:1 + 2 * n_seg]
    o_ref = refs[1 + 2 * n_seg]
    tq = q_ref.shape[1] // n_units
    queries = []
    for u in range(n_units):
        rows = slice(u * tq, (u + 1) * tq)
        heads = [q_ref[0, rows, hd * dq:(hd + 1) * dq] for hd in range(n_heads)]
        queries.append(heads[0] if n_heads == 1 else jnp.concatenate(heads, axis=0))
    outs = _attend_units(queries, _segments(k_refs, KEY_SEG), _segments(v_refs, KEY_SEG))
    for u, o in enumerate(outs):
        for hd in range(n_heads):
            o_ref[0, u * tq:(u + 1) * tq, hd * dv:(hd + 1) * dv] = o[hd * tq:(hd + 1) * tq].astype(BF16)


def _attention(body, q, ks, vs, extra, extra_specs, n_groups, wq, wk, wv, wo, tq, name):
    b, s, _ = q.shape
    in_specs = [pl.BlockSpec((1, tq, wq), lambda bi, gi, qi: (bi, qi, gi))]
    for k in ks:
        in_specs.append(pl.BlockSpec((1, k.shape[1], wk), lambda bi, gi, qi: (bi, 0, gi)))
    for v in vs:
        in_specs.append(pl.BlockSpec((1, v.shape[1], wv), lambda bi, gi, qi: (bi, 0, gi)))
    return pl.pallas_call(
        body,
        grid=(b, n_groups, s // tq),
        in_specs=in_specs + extra_specs,
        out_specs=pl.BlockSpec((1, tq, wo), lambda bi, gi, qi: (bi, qi, gi)),
        out_shape=jax.ShapeDtypeStruct((b, s, n_groups * wo), BF16),
        compiler_params=_params(3),
        name=name,
    )(q, *ks, *vs, *extra)


def _small_spec(a):
    return pl.BlockSpec(a.shape, lambda bi, gi, qi: (0,) * a.ndim)


def _block_tail_kernel(tiles_per_seq, final, *refs):
    (x_ref, xp_ref, xn_ref, o_ref, op_ref, on_ref, mod_ref, g_ref, wo_ref, wup_ref, cw_ref, cb_ref,
     wdn_ref) = refs[:13]
    fg_ref = refs[13] if final else None
    out_ref, act_ref = refs[-2:]
    tm, d = x_ref.shape
    n_ext = tm + 2 * HALO
    pos = pl.program_id(0) % tiles_per_seq
    mod = mod_ref[0]
    x_ext = jnp.concatenate([xp_ref[...], x_ref[...], xn_ref[...]], axis=0)
    o_ext = jnp.concatenate([op_ref[...], o_ref[...], on_ref[...]], axis=0)
    mixed = _dot(o_ext, wo_ref[...])[O_HALO - HALO:O_HALO - HALO + n_ext]
    x1_ext = x_ext + mod[2:3] * mixed
    row = lax.broadcasted_iota(jnp.int32, (n_ext, 1), 0)
    first_kept = jnp.where(pos > 0, 0, HALO)
    end_kept = jnp.where(pos < tiles_per_seq - 1, n_ext, HALO + tm)
    keep = (row >= first_kept) & (row < end_kept)
    h_ext = jnp.where(keep, _modulated(x1_ext, mod, g_ref[...], 3, 4), 0.0).astype(BF16)
    for c in range(wup_ref.shape[0]):
        u = _dot(h_ext, wup_ref[c])
        u_prev = pltpu.roll(u, 1, 0)[HALO:HALO + tm]
        u_next = pltpu.roll(u, n_ext - 1, 0)[HALO:HALO + tm]
        cw = cw_ref[c]
        y = cb_ref[c] + u_prev * cw[0:1] + u[HALO:HALO + tm] * cw[1:2] + u_next * cw[2:3]
        a, gate = y[:, :FF_CHUNK], y[:, FF_CHUNK:]
        act_ref[:, c * FF_CHUNK:(c + 1) * FF_CHUNK] = ((a / (1.0 + jnp.exp(-a))) * gate).astype(BF16)
    out = x1_ext[HALO:HALO + tm] + mod[5:6] * _dot(act_ref[...], wdn_ref[...])
    if final:
        out = _rms(out, fg_ref[...])
    out_ref[...] = out


def _block_tail(x, o, mod, mod_row, g, w_o, w_up, conv_w, conv_b, w_down, tm, seq, final_g=None):
    rows, d = x.shape
    tiles_per_seq = seq // tm
    const2 = lambda i: (0, 0)
    const3 = lambda i: (0, 0, 0)
    once = pl.Buffered(1)

    def halo_specs(width, n_halo):
        per_tile = tm // n_halo
        last = rows // n_halo - 1
        return [pl.BlockSpec((tm, width), lambda i: (i, 0)),
                pl.BlockSpec((n_halo, width), lambda i: (jnp.maximum(i * per_tile - 1, 0), 0)),
                pl.BlockSpec((n_halo, width), lambda i: (jnp.minimum((i + 1) * per_tile, last), 0))]

    in_specs = (halo_specs(d, HALO) + halo_specs(o.shape[1], O_HALO) +
                [pl.BlockSpec((1, 6, d), lambda i: (mod_row(i), 0, 0)),
                 pl.BlockSpec((1, d), const2),
                 pl.BlockSpec(w_o.shape, const2, pipeline_mode=once),
                 pl.BlockSpec(w_up.shape, const3, pipeline_mode=once),
                 pl.BlockSpec(conv_w.shape, const3),
                 pl.BlockSpec(conv_b.shape, const3),
                 pl.BlockSpec(w_down.shape, const2, pipeline_mode=once)])
    args = [x, x, x, o, o, o, mod, g.reshape(1, d), w_o, w_up, conv_w, conv_b, w_down]
    if final_g is not None:
        in_specs.append(pl.BlockSpec((1, d), const2))
        args.append(final_g.reshape(1, d))
    return pl.pallas_call(
        functools.partial(_block_tail_kernel, tiles_per_seq, final_g is not None),
        grid=(rows // tm,),
        in_specs=in_specs,
        out_specs=pl.BlockSpec((tm, d), lambda i: (i, 0)),
        out_shape=jax.ShapeDtypeStruct((rows, d), F32),
        scratch_shapes=[pltpu.VMEM((tm, w_down.shape[0]), BF16)],
        compiler_params=_params(1),
        name="block_tail",
    )(*args)


def _ffn_weights(w_up, conv_w, conv_b, w_down):
    d, two_ff = w_up.shape
    ff = two_ff // 2
    n = ff // FF_CHUNK

    def regroup(t):
        a = t[..., :ff].reshape(t.shape[:-1] + (n, FF_CHUNK))
        g = t[..., ff:].reshape(t.shape[:-1] + (n, FF_CHUNK))
        return jnp.moveaxis(jnp.concatenate([a, g], axis=-1), -2, 0)

    return (regroup(w_up).astype(BF16), regroup(conv_w), regroup(conv_b.reshape(1, two_ff)),
            w_down.astype(BF16))


def kernel(x, c, ctx, c_ctx, ada_w, ada_b, norm1_g, norm2_g, ffn_up, ffn_conv_w, ffn_conv_b, ffn_down,
           a_w_qkv, a_w_o, a_lambda_q1, a_lambda_k1, a_lambda_q2, a_lambda_k2, a_subln_g,
           b_w_qkv, b_w_o, b_sink,
           c_w_down, c_q_norm_g, c_kv_norm_g, c_w_uq, c_w_ukv, c_w_o,
           d_w_qkv, d_q_norm_g, d_k_norm_g, d_w_o,
           final_g):
    batch, seq, d = x.shape
    n_ctx = ctx.shape[1]
    depth = ada_w.shape[0]
    n_mixers = 4

    mods = _ada_mods(c, c_ctx, ada_w, ada_b)
    rope64 = _rope_tables(seq, 64)
    rope128 = _rope_tables(seq, 128)

    tm_lat = min(512, seq)
    tm_ctx = n_ctx
    tq_lat = {0: 1024, 1: 256, 2: 2048, 3: 1024}
    tq_ctx = n_ctx
    lat_row = lambda i: i // (seq // tm_lat)
    ctx_row = lambda i: batch

    xl = x.reshape(batch * seq, d)
    xc = ctx.reshape(batch * n_ctx, d)

    def to3(t, n):
        return t.reshape(batch, n, t.shape[-1])

    for i in range(depth):
        kind, r = i % n_mixers, i // n_mixers
        need_ctx = i < depth - 1
        mod = mods[i]
        g1 = norm1_g[i]

        if kind == 0:
            width = 2 * A_HEADS * A_HEAD_DIM
            weights = [a_w_qkv[r].astype(BF16)]
            widths = [width, width, width]
            body, tabs = _proj_a_kernel, rope64
        elif kind == 1:
            nq, nkv = B_HEADS * B_HEAD_DIM, B_KV_HEADS * B_HEAD_DIM
            group = B_HEADS // B_KV_HEADS
            order = [(2 * p + half) * group + g for p in range(B_KV_HEADS // 2)
                     for g in range(group) for half in range(2)]
            perm = jnp.asarray([hq * B_HEAD_DIM + j for hq in order for j in range(B_HEAD_DIM)])
            w = b_w_qkv[r]
            weights = [jnp.concatenate([w[:, :nq][:, perm], w[:, nq:]], axis=1).astype(BF16)]
            widths = [nq, nkv, nkv]
            body, tabs = _proj_b_kernel, rope64
        elif kind == 2:
            wd = jnp.pad(c_w_down[r], ((0, 0), (0, LANES - C_ROPE_DIM)))
            wuq = c_w_uq[r].reshape(C_Q_RANK, C_HEADS, C_NOPE_DIM + C_ROPE_DIM)
            wuq = jnp.pad(wuq, ((0, 0), (0, 0), (0, 2 * LANES - C_NOPE_DIM - C_ROPE_DIM)))
            weights = [wd.astype(BF16), c_q_norm_g[r].reshape(1, -1), c_kv_norm_g[r].reshape(1, -1),
                       wuq.reshape(C_Q_RANK, C_HEADS * 2 * LANES).astype(BF16), c_w_ukv[r].astype(BF16)]
            widths = [C_HEADS * 2 * LANES, C_HEADS * 2 * LANES, C_HEADS * C_V_DIM]
            body, tabs = _proj_c_kernel, rope64
        else:
            weights = [d_w_qkv[r].astype(BF16), d_q_norm_g[r].reshape(1, -1), d_k_norm_g[r].reshape(1, -1)]
            widths = [D_HEADS * D_HEAD_DIM, D_KV_HEADS * D_HEAD_DIM, D_KV_HEADS * D_HEAD_DIM]
            body, tabs = _proj_d_kernel, rope128

        ql, kl, vl = _project(body, xl, mod, lat_row, g1, tabs, weights, widths, tm_lat, seq)
        qc, kc, vc = _project(body, xc, mod, ctx_row, g1, None, weights, widths, tm_ctx, n_ctx)
        ql, kl, vl = to3(ql, seq), to3(kl, seq), to3(vl, seq)
        qc, kc, vc = to3(qc, n_ctx), to3(kc, n_ctx), to3(vc, n_ctx)

        def attend(q, ks, vs, tq, n_units):
            n_seg = len(ks)
            if kind == 0:
                lambda_init = 0.8 - 0.6 * math.exp(-0.3 * i)
                extra = [a_lambda_q1[r].reshape(1, -1), a_lambda_k1[r].reshape(1, -1),
                         a_lambda_q2[r].reshape(1, -1), a_lambda_k2[r].reshape(1, -1),
                         a_subln_g[r].reshape(1, -1)]
                return _attention(functools.partial(_attn_a_kernel, n_seg, n_units, lambda_init), q, ks, vs,
                                  extra, [_small_spec(e) for e in extra], A_HEADS, LANES, LANES, LANES, LANES,
                                  tq, "attn_a")
            if kind == 1:
                group = B_HEADS // B_KV_HEADS
                return _attention(functools.partial(_attn_b_kernel, n_seg, tq, seq), q, ks, vs, [b_sink[r]],
                                  [pl.BlockSpec(memory_space=pltpu.SMEM)], B_KV_HEADS // 2,
                                  group * LANES, LANES, LANES, group * LANES, tq, "attn_b")
            if kind == 2:
                return _attention(functools.partial(_attn_plain_kernel, n_seg, n_units, 1, 2 * LANES, C_V_DIM),
                                  q, ks, vs, [], [], C_HEADS, 2 * LANES, 2 * LANES, C_V_DIM, C_V_DIM,
                                  tq, "attn_c")
            group = D_HEADS // D_KV_HEADS
            return _attention(functools.partial(_attn_plain_kernel, n_seg, n_units, group, D_HEAD_DIM,
                                                D_HEAD_DIM),
                              q, ks, vs, [], [], D_KV_HEADS, group * D_HEAD_DIM, D_HEAD_DIM, D_HEAD_DIM,
                              group * D_HEAD_DIM, tq, "attn_d")

        if kind == 0:
            w_o = a_w_o[r]
        elif kind == 1:
            w_o = b_w_o[r].reshape(B_HEADS, B_HEAD_DIM, d)[jnp.asarray(order)].reshape(-1, d)
        elif kind == 2:
            w_o = c_w_o[r]
        else:
            w_o = d_w_o[r]
        w_o = w_o.astype(BF16)
        ffn_w = _ffn_weights(ffn_up[i], ffn_conv_w[i], ffn_conv_b[i], ffn_down[i])

        tq = min(tq_lat[kind], seq)
        ol = attend(ql, [kc, kl], [vc, vl], tq, 4 if tq >= 1024 else 1).reshape(batch * seq, -1)
        xl = _block_tail(xl, ol, mod, lat_row, norm2_g[i], w_o, *ffn_w, tm_lat, seq,
                         final_g=None if need_ctx else final_g)
        if need_ctx:
            oc = attend(qc, [kc], [vc], tq_ctx, 1).reshape(batch * n_ctx, -1)
            xc = _block_tail(xc, oc, mod, ctx_row, norm2_g[i], w_o, *ffn_w, tm_ctx, n_ctx)

    return xl.reshape(batch, seq, d)
```

```python
import functools
import math

import jax
import jax.numpy as jnp
from jax import lax
from jax.experimental import pallas as pl
from jax.experimental.pallas import tpu as pltpu

F32 = jnp.float32
BF16 = jnp.bfloat16

GRID_W = 64
ROPE_THETA = 10000.0
NORM_EPS = 1e-6
LOG2E = math.log2(math.e)

A_HEADS, A_HEAD_DIM = 8, 64
B_HEADS, B_KV_HEADS, B_HEAD_DIM, B_WINDOW = 16, 4, 64, 128
C_HEADS, C_Q_RANK, C_KV_RANK, C_NOPE_DIM, C_ROPE_DIM, C_V_DIM = 8, 384, 256, 128, 64, 128
D_HEADS, D_KV_HEADS, D_HEAD_DIM = 8, 4, 128
CONV_W = 3

LANES = 128
SUBLANES = 8
MOD_ROWS = 16
VMEM_LIMIT = 56 * 1024 * 1024
FF_CHUNK = 256
HALO = SUBLANES
O_HALO = 2 * SUBLANES
KEY_SEG = 512
B_SUB_TILE = 128
PROJ_SUB_ROWS = 256


def _params(n_axes):
    return pltpu.CompilerParams(dimension_semantics=("arbitrary",) * n_axes,
                                vmem_limit_bytes=VMEM_LIMIT)


def _dot(a, b):
    return jnp.dot(a, b, preferred_element_type=F32)


def _dot_nt(a, b):
    return lax.dot_general(a, b, (((1,), (1,)), ((), ())), preferred_element_type=F32)


def _lane_iota(n=LANES):
    return lax.broadcasted_iota(jnp.int32, (1, n), 1)


def _ada_kernel(c_ref, w_ref, b_ref, o_ref):
    cv = c_ref[...]
    s = cv / (1.0 + jnp.exp(-cv))
    o_ref[0] = _dot(s.astype(BF16), w_ref[0].astype(BF16)) + b_ref[0]


def _ada_mods(c, c_ctx, ada_w, ada_b):
    n_layers, d, n_out = ada_w.shape
    batch = c.shape[0]
    rows = jnp.zeros((MOD_ROWS, d), F32).at[:batch].set(c).at[batch].set(c_ctx)
    tn = 1536
    out = pl.pallas_call(
        _ada_kernel,
        grid=(n_layers, n_out // tn),
        in_specs=[pl.BlockSpec((MOD_ROWS, d), lambda l, j: (0, 0)),
                  pl.BlockSpec((1, d, tn), lambda l, j: (l, 0, j)),
                  pl.BlockSpec((1, 1, tn), lambda l, j: (l, 0, j))],
        out_specs=pl.BlockSpec((1, MOD_ROWS, tn), lambda l, j: (l, 0, j)),
        out_shape=jax.ShapeDtypeStruct((n_layers, MOD_ROWS, n_out), F32),
        compiler_params=_params(2),
        name="ada_mods",
    )(rows, ada_w, ada_b.reshape(n_layers, 1, n_out))
    return out.reshape(n_layers, MOD_ROWS, 6, d)


def _modulated(x, mod, g, shift_row, scale_row):
    ms = jnp.mean(x * x, axis=-1, keepdims=True)
    y = x * lax.rsqrt(ms + NORM_EPS)
    return (y * g) * (1.0 + mod[scale_row:scale_row + 1]) + mod[shift_row:shift_row + 1]


def _rms(x, g):
    ms = jnp.mean(x * x, axis=-1, keepdims=True)
    return x * lax.rsqrt(ms + NORM_EPS) * g


def _rope64(xc, cos, sin, lo32):
    r_dn = pltpu.roll(xc, 32, 1)
    r_up = pltpu.roll(xc, 96, 1)
    return xc * cos + jnp.where(lo32, r_up, r_dn) * sin


def _rope128(xc, cos, sin):
    return xc * cos + pltpu.roll(xc, 64, 1) * sin


def _rope_tables(n_tokens, rot_dim):
    n_rows = n_tokens // GRID_W
    rows = jnp.repeat(jnp.arange(n_rows, dtype=F32), GRID_W)
    cols = jnp.tile(jnp.arange(GRID_W, dtype=F32), n_rows)
    n_freq = rot_dim // 4
    inv_freq = ROPE_THETA ** (-jnp.arange(n_freq, dtype=F32) / n_freq)
    ang = jnp.concatenate([rows[:, None] * inv_freq, cols[:, None] * inv_freq], axis=-1)
    cos, sin = jnp.cos(ang), jnp.sin(ang)
    reps = LANES // rot_dim
    return (jnp.tile(jnp.concatenate([cos, cos], axis=-1), (1, reps)),
            jnp.tile(jnp.concatenate([-sin, sin], axis=-1), (1, reps)))


def _lane_blocks(parts):
    return [s[:, j * LANES:(j + 1) * LANES] for s in parts for j in range(s.shape[1] // LANES)]


def _softmax_pv(scores, values, sink=None):
    m = jnp.max(functools.reduce(jnp.maximum, _lane_blocks(scores)), axis=-1, keepdims=True)
    if sink is not None:
        r = m.shape[0] // len(sink)
        m = jnp.concatenate([jnp.maximum(m[j * r:(j + 1) * r], sj) for j, sj in enumerate(sink)], axis=0)
    o = functools.reduce(jnp.add, [_dot(jnp.exp2(s - m).astype(BF16), v) for s, v in zip(scores, values)])
    l = o[:, LANES:]
    if sink is not None:
        l = l + jnp.concatenate([jnp.exp2(sj - m[j * r:(j + 1) * r]) for j, sj in enumerate(sink)], axis=0)
    return o[:, :LANES] / l


def _add_row_tiled(s, bias):
    r = bias.shape[0]
    return jnp.concatenate([s[j * r:(j + 1) * r] + bias for j in range(s.shape[0] // r)], axis=0)


def _with_ones(values):
    assert all(v.shape[1] == LANES for v in values)
    return [jnp.concatenate([v, jnp.ones_like(v)], axis=1) for v in values]


def _attend_units(units):
    all_scores = []
    for q, keys, _, biases, _ in units:
        scores = [_dot_nt(q, k) for k in keys]
        if biases is not None:
            scores = [s if b is None else _add_row_tiled(s, b) for s, b in zip(scores, biases)]
        all_scores.append(scores)
    return [_softmax_pv(scores, unit[2], unit[4]) for scores, unit in zip(all_scores, units)]


def _segments(refs, seg, lanes):
    out = []
    for r in refs:
        n = r.shape[1]
        if n <= seg:
            out.append(r[0, :, lanes])
        else:
            out += [r[0, j * seg:(j + 1) * seg, lanes] for j in range(n // seg)]
    return out


def _proj_a_kernel(rope, *refs):
    if rope:
        x_ref, mod_ref, g_ref, cos_ref, sin_ref, w_ref, q_ref, k_ref, v_ref = refs
    else:
        x_ref, mod_ref, g_ref, w_ref, q_ref, k_ref, v_ref = refs
    h = _modulated(x_ref[...], mod_ref[0], g_ref[...], 0, 1).astype(BF16)
    qkv = _dot(h, w_ref[...])
    width = q_ref.shape[1]
    q_scale = A_HEAD_DIM ** -0.5 * LOG2E
    lo32 = (_lane_iota() % 64) < 32
    for c in range(width // LANES):
        sl = slice(c * LANES, (c + 1) * LANES)
        qc = qkv[:, sl]
        kc = qkv[:, width + c * LANES: width + (c + 1) * LANES]
        if rope:
            qc = _rope64(qc, cos_ref[...], sin_ref[...], lo32)
            kc = _rope64(kc, cos_ref[...], sin_ref[...], lo32)
        q_ref[:, sl] = (qc * q_scale).astype(BF16)
        k_ref[:, sl] = kc.astype(BF16)
    v_ref[...] = qkv[:, 2 * width:].astype(BF16)


def _proj_b_kernel(rope, *refs):
    if rope:
        x_ref, mod_ref, g_ref, cos_ref, sin_ref, w_ref, q_ref, k_ref, v_ref = refs
    else:
        x_ref, mod_ref, g_ref, w_ref, q_ref, k_ref, v_ref = refs
    h = _modulated(x_ref[...], mod_ref[0], g_ref[...], 0, 1).astype(BF16)
    qkv = _dot(h, w_ref[...])
    nq, nkv = q_ref.shape[1], k_ref.shape[1]
    q_scale = B_HEAD_DIM ** -0.5 * LOG2E
    lo32 = (_lane_iota() % 64) < 32
    for c in range(nq // LANES):
        sl = slice(c * LANES, (c + 1) * LANES)
        qc = qkv[:, sl]
        if rope:
            qc = _rope64(qc, cos_ref[...], sin_ref[...], lo32)
        q_ref[:, sl] = (qc * q_scale).astype(BF16)
    for c in range(nkv // LANES):
        sl = slice(c * LANES, (c + 1) * LANES)
        kc = qkv[:, nq + c * LANES: nq + (c + 1) * LANES]
        if rope:
            kc = _rope64(kc, cos_ref[...], sin_ref[...], lo32)
        k_ref[:, sl] = kc.astype(BF16)
    v_ref[...] = qkv[:, nq + nkv:].astype(BF16)


def _proj_c_kernel(rope, *refs):
    if rope:
        (x_ref, mod_ref, g_ref, cos_ref, sin_ref, wd_ref, qg_ref, kvg_ref, wuq_ref, wukv_ref,
         q_ref, k_ref, v_ref) = refs
    else:
        x_ref, mod_ref, g_ref, wd_ref, qg_ref, kvg_ref, wuq_ref, wukv_ref, q_ref, k_ref, v_ref = refs
    h = _modulated(x_ref[...], mod_ref[0], g_ref[...], 0, 1).astype(BF16)
    down = _dot(h, wd_ref[...])
    cq = _rms(down[:, :C_Q_RANK], qg_ref[...]).astype(BF16)
    ckv = _rms(down[:, C_Q_RANK:C_Q_RANK + C_KV_RANK], kvg_ref[...]).astype(BF16)
    k_rope = down[:, C_Q_RANK + C_KV_RANK:]
    q = _dot(cq, wuq_ref[...])
    kv = _dot(ckv, wukv_ref[...])
    q_scale = (C_NOPE_DIM + C_ROPE_DIM) ** -0.5 * LOG2E
    lo32 = (_lane_iota() % 64) < 32
    if rope:
        k_rope = _rope64(k_rope, cos_ref[...], sin_ref[...], lo32)
    k_rope = k_rope.astype(BF16)
    for hd in range(C_HEADS):
        lo = slice(2 * hd * LANES, (2 * hd + 1) * LANES)
        hi = slice((2 * hd + 1) * LANES, (2 * hd + 2) * LANES)
        q_rope = q[:, hi]
        if rope:
            q_rope = _rope64(q_rope, cos_ref[...], sin_ref[...], lo32)
        q_ref[:, lo] = (q[:, lo] * q_scale).astype(BF16)
        q_ref[:, hi] = (q_rope * q_scale).astype(BF16)
        k_ref[:, lo] = kv[:, lo].astype(BF16)
        k_ref[:, hi] = k_rope
        v_ref[:, hd * LANES:(hd + 1) * LANES] = kv[:, hi].astype(BF16)


def _proj_d_kernel(rope, *refs):
    if rope:
        x_ref, mod_ref, g_ref, cos_ref, sin_ref, w_ref, qg_ref, kg_ref, q_ref, k_ref, v_ref = refs
    else:
        x_ref, mod_ref, g_ref, w_ref, qg_ref, kg_ref, q_ref, k_ref, v_ref = refs
    nq, nkv = q_ref.shape[1], k_ref.shape[1]
    q_scale = D_HEAD_DIM ** -0.5 * LOG2E
    tm = x_ref.shape[0]
    sub = min(tm, PROJ_SUB_ROWS)
    for r0 in range(0, tm, sub):
        rows = slice(r0, r0 + sub)
        h = _modulated(x_ref[rows, :], mod_ref[0], g_ref[...], 0, 1).astype(BF16)
        qkv = _dot(h, w_ref[...])
        if rope:
            cos, sin = cos_ref[rows, :], sin_ref[rows, :]
        for c in range(nq // LANES):
            sl = slice(c * LANES, (c + 1) * LANES)
            qc = _rms(qkv[:, sl], qg_ref[...])
            if rope:
                qc = _rope128(qc, cos, sin)
            q_ref[rows, sl] = (qc * q_scale).astype(BF16)
        for c in range(nkv // LANES):
            sl = slice(c * LANES, (c + 1) * LANES)
            kc = _rms(qkv[:, nq + c * LANES: nq + (c + 1) * LANES], kg_ref[...])
            if rope:
                kc = _rope128(kc, cos, sin)
            k_ref[rows, sl] = kc.astype(BF16)
        v_ref[rows, :] = qkv[:, nq + nkv:].astype(BF16)


def _project(body, x, mod, mod_row, g, rope_tabs, weights, out_widths, tm, seq):
    rows, d = x.shape
    tiles_per_seq = seq // tm
    const = lambda i: (0, 0)
    in_specs = [pl.BlockSpec((tm, d), lambda i: (i, 0)),
                pl.BlockSpec((1, 6, d), lambda i: (mod_row(i), 0, 0)),
                pl.BlockSpec((1, d), const)]
    args = [x, mod, g.reshape(1, d)]
    if rope_tabs is not None:
        in_specs += [pl.BlockSpec((tm, LANES), lambda i: (i % tiles_per_seq, 0))] * 2
        args += list(rope_tabs)
    for w in weights:
        in_specs.append(pl.BlockSpec(w.shape, const))
        args.append(w)
    return pl.pallas_call(
        functools.partial(body, rope_tabs is not None),
        grid=(rows // tm,),
        in_specs=in_specs,
        out_specs=[pl.BlockSpec((tm, w), lambda i: (i, 0)) for w in out_widths],
        out_shape=[jax.ShapeDtypeStruct((rows, w), BF16) for w in out_widths],
        compiler_params=_params(1),
        name=body.__name__.strip("_"),
    )(*args)


def _attn_a_kernel(n_seg, n_units, lambda_init, *refs, gps):
    q_ref = refs[0]
    k_refs = refs[1:1 + n_seg]
    v_refs = refs[1 + n_seg:1 + 2 * n_seg]
    lq1_ref, lk1_ref, lq2_ref, lk2_ref, g_ref, o_ref = refs[1 + 2 * n_seg:]
    lam = (jnp.exp(jnp.sum(lq1_ref[...] * lk1_ref[...], keepdims=True))
           - jnp.exp(jnp.sum(lq2_ref[...] * lk2_ref[...], keepdims=True)) + lambda_init)
    tq = q_ref.shape[1] // n_units
    lane = _lane_iota()
    zero = jnp.zeros((tq, LANES), BF16)
    units = []
    for g in range(gps):
        lanes = slice(g * LANES, (g + 1) * LANES)
        keys, values = _segments(k_refs, KEY_SEG, lanes), _with_ones(_segments(v_refs, KEY_SEG, lanes))
        for u in range(n_units):
            q = q_ref[0, u * tq:(u + 1) * tq, lanes]
            qs = jnp.concatenate([jnp.where(lane < 64, q, zero), jnp.where(lane >= 64, q, zero)], axis=0)
            units.append((qs, keys, values, None, None))
    outs = _attend_units(units)
    for i, o in enumerate(outs):
        g, u = divmod(i, n_units)
        diff = o[:tq] - lam * o[tq:]
        o_ref[0, u * tq:(u + 1) * tq, g * LANES:(g + 1) * LANES] = (
            _rms(diff, g_ref[...]) * (1.0 - lambda_init)).astype(BF16)


def _attn_b_kernel(n_seg, tq, n_lat, *refs, gps):
    q_ref = refs[0]
    k_refs = refs[1:1 + n_seg]
    v_refs = refs[1 + n_seg:1 + 2 * n_seg]
    sink_ref, o_ref = refs[1 + 2 * n_seg:]
    lane = _lane_iota()
    group = B_HEADS // B_KV_HEADS
    n_sub = q_ref.shape[1] // tq
    units = []
    for g in range(gps):
        pair = pl.program_id(1) * gps + g
        lanes = slice(g * LANES, (g + 1) * LANES)
        kc, vc = k_refs[0][0, :, lanes], _with_ones([v_refs[0][0, :, lanes]])[0]
        for t in range(n_sub):
            keys, values, biases = [kc], [vc], None
            if n_seg == 2:
                span = tq + 2 * B_WINDOW
                start = (pl.program_id(2) * n_sub + t) * tq
                k0 = pl.multiple_of(jnp.clip(start - B_WINDOW, 0, n_lat - span), LANES)
                keys.append(k_refs[1][0, pl.ds(k0, span), lanes])
                values += _with_ones([v_refs[1][0, pl.ds(k0, span), lanes]])
                q_pos = start + lax.broadcasted_iota(jnp.int32, (tq, 1), 0)
                k_pos = k0 + lax.broadcasted_iota(jnp.int32, (1, span), 1)
                biases = [None, jnp.where(jnp.abs(q_pos - k_pos) <= B_WINDOW, 0.0, -jnp.inf)]
            rows = slice(t * tq, (t + 1) * tq)
            for half in range(2):
                keep = (lane >= 64) if half else (lane < 64)
                q = jnp.concatenate(
                    [jnp.where(keep, q_ref[0, rows, (g * group + c) * LANES:(g * group + c + 1) * LANES],
                               jnp.zeros((tq, LANES), BF16)) for c in range(group)], axis=0)
                sink = [sink_ref[(2 * pair + half) * group + c] for c in range(group)]
                units.append((q, keys, values, biases, sink))
    outs = _attend_units(units)
    for g in range(gps):
        for t in range(n_sub):
            lo, hi = outs[2 * (g * n_sub + t)], outs[2 * (g * n_sub + t) + 1]
            for c in range(group):
                blk = slice(c * tq, (c + 1) * tq)
                o_ref[0, t * tq:(t + 1) * tq, (g * group + c) * LANES:(g * group + c + 1) * LANES] = (
                    jnp.where(lane < 64, lo[blk], hi[blk]).astype(BF16))


def _attn_plain_kernel(n_seg, n_units, n_heads, dq, dv, *refs, gps):
    q_ref = refs[0]
    k_refs = refs[1:1 + n_seg]
    v_refs = refs[1 + n_seg:1 + 2 * n_seg]
    o_ref = refs[1 + 2 * n_seg]
    tq = q_ref.shape[1] // n_units
    units = []
    for g in range(gps):
        keys = _segments(k_refs, KEY_SEG, slice(g * dq, (g + 1) * dq))
        values = _with_ones(_segments(v_refs, KEY_SEG, slice(g * dv, (g + 1) * dv)))
        for u in range(n_units):
            rows = slice(u * tq, (u + 1) * tq)
            heads = [q_ref[0, rows, (g * n_heads + hd) * dq:(g * n_heads + hd + 1) * dq]
                     for hd in range(n_heads)]
            units.append((heads[0] if n_heads == 1 else jnp.concatenate(heads, axis=0),
                          keys, values, None, None))
    outs = _attend_units(units)
    for i, o in enumerate(outs):
        g, u = divmod(i, n_units)
        for hd in range(n_heads):
            o_ref[0, u * tq:(u + 1) * tq, (g * n_heads + hd) * dv:(g * n_heads + hd + 1) * dv] = (
                o[hd * tq:(hd + 1) * tq].astype(BF16))


def _attention(body, q, ks, vs, extra, extra_specs, n_groups, gps, wq, wk, wv, wo, tq, name):
    b, s, _ = q.shape
    wq, wk, wv, wo = gps * wq, gps * wk, gps * wv, gps * wo
    in_specs = [pl.BlockSpec((1, tq, wq), lambda bi, gi, qi: (bi, qi, gi))]
    for k in ks:
        in_specs.append(pl.BlockSpec((1, k.shape[1], wk), lambda bi, gi, qi: (bi, 0, gi)))
    for v in vs:
        in_specs.append(pl.BlockSpec((1, v.shape[1], wv), lambda bi, gi, qi: (bi, 0, gi)))
    return pl.pallas_call(
        functools.partial(body, gps=gps),
        grid=(b, n_groups // gps, s // tq),
        in_specs=in_specs + extra_specs,
        out_specs=pl.BlockSpec((1, tq, wo), lambda bi, gi, qi: (bi, qi, gi)),
        out_shape=jax.ShapeDtypeStruct((b, s, (n_groups // gps) * wo), BF16),
        compiler_params=_params(3),
        name=name,
    )(q, *ks, *vs, *extra)


def _small_spec(a):
    return pl.BlockSpec(a.shape, lambda bi, gi, qi: (0,) * a.ndim)


def _block_tail_kernel(tiles_per_seq, final, *refs):
    (x_ref, xp_ref, xn_ref, o_ref, op_ref, on_ref, mod_ref, g_ref, wo_ref, wup_ref, cw_ref, cb_ref,
     wdn_ref) = refs[:13]
    fg_ref = refs[13] if final else None
    out_ref, act_ref = refs[-2:]
    tm, d = x_ref.shape
    n_ext = tm + 2 * HALO
    pos = pl.program_id(0) % tiles_per_seq
    mod = mod_ref[0]
    x_ext = jnp.concatenate([xp_ref[...], x_ref[...], xn_ref[...]], axis=0)
    o_ext = jnp.concatenate([op_ref[...], o_ref[...], on_ref[...]], axis=0)
    mixed = _dot(o_ext, wo_ref[...])[O_HALO - HALO:O_HALO - HALO + n_ext]
    x1_ext = x_ext + mod[2:3] * mixed
    row = lax.broadcasted_iota(jnp.int32, (n_ext, 1), 0)
    first_kept = jnp.where(pos > 0, 0, HALO)
    end_kept = jnp.where(pos < tiles_per_seq - 1, n_ext, HALO + tm)
    keep = (row >= first_kept) & (row < end_kept)
    h_ext = jnp.where(keep, _modulated(x1_ext, mod, g_ref[...], 3, 4), 0.0).astype(BF16)
    ff = wdn_ref.shape[0]

    def conv_branch(cols):
        u = _dot(h_ext, wup_ref[:, cols])
        u_prev = pltpu.roll(u, 1, 0)[HALO:HALO + tm]
        u_next = pltpu.roll(u, n_ext - 1, 0)[HALO:HALO + tm]
        return (cb_ref[:, cols] + u_prev * cw_ref[0:1, cols] + u[HALO:HALO + tm] * cw_ref[1:2, cols]
                + u_next * cw_ref[2:3, cols])

    for c in range(ff // FF_CHUNK):
        a = conv_branch(slice(c * FF_CHUNK, (c + 1) * FF_CHUNK))
        gate = conv_branch(slice(ff + c * FF_CHUNK, ff + (c + 1) * FF_CHUNK))
        act_ref[:, c * FF_CHUNK:(c + 1) * FF_CHUNK] = ((a / (1.0 + jnp.exp(-a))) * gate).astype(BF16)
    out = x1_ext[HALO:HALO + tm] + mod[5:6] * _dot(act_ref[...], wdn_ref[...])
    if final:
        out = _rms(out, fg_ref[...])
    out_ref[...] = out


def _block_tail(x, o, mod, mod_row, g, w_o, w_up, conv_w, conv_b, w_down, tm, seq, final_g=None):
    rows, d = x.shape
    tiles_per_seq = seq // tm
    const2 = lambda i: (0, 0)
    once = pl.Buffered(1)

    def halo_specs(width, n_halo):
        per_tile = tm // n_halo
        last = rows // n_halo - 1
        return [pl.BlockSpec((tm, width), lambda i: (i, 0)),
                pl.BlockSpec((n_halo, width), lambda i: (jnp.maximum(i * per_tile - 1, 0), 0)),
                pl.BlockSpec((n_halo, width), lambda i: (jnp.minimum((i + 1) * per_tile, last), 0))]

    in_specs = (halo_specs(d, HALO) + halo_specs(o.shape[1], O_HALO) +
                [pl.BlockSpec((1, 6, d), lambda i: (mod_row(i), 0, 0)),
                 pl.BlockSpec((1, d), const2),
                 pl.BlockSpec(w_o.shape, const2, pipeline_mode=once),
                 pl.BlockSpec(w_up.shape, const2, pipeline_mode=once),
                 pl.BlockSpec(conv_w.shape, const2),
                 pl.BlockSpec(conv_b.shape, const2),
                 pl.BlockSpec(w_down.shape, const2, pipeline_mode=once)])
    args = [x, x, x, o, o, o, mod, g.reshape(1, d), w_o, w_up, conv_w, conv_b, w_down]
    if final_g is not None:
        in_specs.append(pl.BlockSpec((1, d), const2))
        args.append(final_g.reshape(1, d))
    return pl.pallas_call(
        functools.partial(_block_tail_kernel, tiles_per_seq, final_g is not None),
        grid=(rows // tm,),
        in_specs=in_specs,
        out_specs=pl.BlockSpec((tm, d), lambda i: (i, 0)),
        out_shape=jax.ShapeDtypeStruct((rows, d), F32),
        scratch_shapes=[pltpu.VMEM((tm, w_down.shape[0]), BF16)],
        compiler_params=_params(1),
        name="block_tail",
    )(*args)


def _ffn_weights(w_up, conv_w, conv_b, w_down):
    return w_up.astype(BF16), conv_w, conv_b.reshape(1, -1), w_down.astype(BF16)


def kernel(x, c, ctx, c_ctx, ada_w, ada_b, norm1_g, norm2_g, ffn_up, ffn_conv_w, ffn_conv_b, ffn_down,
           a_w_qkv, a_w_o, a_lambda_q1, a_lambda_k1, a_lambda_q2, a_lambda_k2, a_subln_g,
           b_w_qkv, b_w_o, b_sink,
           c_w_down, c_q_norm_g, c_kv_norm_g, c_w_uq, c_w_ukv, c_w_o,
           d_w_qkv, d_q_norm_g, d_k_norm_g, d_w_o,
           final_g):
    batch, seq, d = x.shape
    n_ctx = ctx.shape[1]
    depth = ada_w.shape[0]
    n_mixers = 4

    mods = _ada_mods(c, c_ctx, ada_w, ada_b)
    rope64 = _rope_tables(seq, 64)
    rope128 = _rope_tables(seq, 128)

    tm_lat = min(512, seq)
    tm_ctx = n_ctx
    tq_lat = {0: 1024, 1: 512, 2: 2048, 3: 1024}
    tq_ctx = n_ctx
    lat_row = lambda i: i // (seq // tm_lat)
    ctx_row = lambda i: batch

    xl = x.reshape(batch * seq, d)
    xc = ctx.reshape(batch * n_ctx, d)

    def to3(t, n):
        return t.reshape(batch, n, t.shape[-1])

    for i in range(depth):
        kind, r = i % n_mixers, i // n_mixers
        need_ctx = i < depth - 1
        mod = mods[i]
        g1 = norm1_g[i]

        if kind == 0:
            width = 2 * A_HEADS * A_HEAD_DIM
            weights = [a_w_qkv[r].astype(BF16)]
            widths = [width, width, width]
            body, tabs = _proj_a_kernel, rope64
        elif kind == 1:
            nq, nkv = B_HEADS * B_HEAD_DIM, B_KV_HEADS * B_HEAD_DIM
            group = B_HEADS // B_KV_HEADS
            w = b_w_qkv[r]
            w_q = w[:, :nq].reshape(d, B_KV_HEADS // 2, 2, group, B_HEAD_DIM)
            w_q = jnp.swapaxes(w_q, 2, 3).reshape(d, nq)
            weights = [jnp.concatenate([w_q, w[:, nq:]], axis=1).astype(BF16)]
            widths = [nq, nkv, nkv]
            body, tabs = _proj_b_kernel, rope64
        elif kind == 2:
            wd = jnp.pad(c_w_down[r], ((0, 0), (0, LANES - C_ROPE_DIM)))
            wuq = c_w_uq[r].reshape(C_Q_RANK, C_HEADS, C_NOPE_DIM + C_ROPE_DIM)
            wuq = jnp.pad(wuq, ((0, 0), (0, 0), (0, 2 * LANES - C_NOPE_DIM - C_ROPE_DIM)))
            weights = [wd.astype(BF16), c_q_norm_g[r].reshape(1, -1), c_kv_norm_g[r].reshape(1, -1),
                       wuq.reshape(C_Q_RANK, C_HEADS * 2 * LANES).astype(BF16), c_w_ukv[r].astype(BF16)]
            widths = [C_HEADS * 2 * LANES, C_HEADS * 2 * LANES, C_HEADS * C_V_DIM]
            body, tabs = _proj_c_kernel, rope64
        else:
            weights = [d_w_qkv[r].astype(BF16), d_q_norm_g[r].reshape(1, -1), d_k_norm_g[r].reshape(1, -1)]
            widths = [D_HEADS * D_HEAD_DIM, D_KV_HEADS * D_HEAD_DIM, D_KV_HEADS * D_HEAD_DIM]
            body, tabs = _proj_d_kernel, rope128

        ql, kl, vl = _project(body, xl, mod, lat_row, g1, tabs, weights, widths, tm_lat, seq)
        qc, kc, vc = _project(body, xc, mod, ctx_row, g1, None, weights, widths, tm_ctx, n_ctx)
        ql, kl, vl = to3(ql, seq), to3(kl, seq), to3(vl, seq)
        qc, kc, vc = to3(qc, n_ctx), to3(kc, n_ctx), to3(vc, n_ctx)

        def attend(q, ks, vs, tq, n_units, all_groups):
            n_seg = len(ks)
            if kind == 0:
                lambda_init = 0.8 - 0.6 * math.exp(-0.3 * i)
                extra = [a_lambda_q1[r].reshape(1, -1), a_lambda_k1[r].reshape(1, -1),
                         a_lambda_q2[r].reshape(1, -1), a_lambda_k2[r].reshape(1, -1),
                         a_subln_g[r].reshape(1, -1)]
                return _attention(functools.partial(_attn_a_kernel, n_seg, n_units, lambda_init), q, ks, vs,
                                  extra, [_small_spec(e) for e in extra], A_HEADS,
                                  A_HEADS if all_groups else 1, LANES, LANES, LANES, LANES, tq, "attn_a")
            if kind == 1:
                group = B_HEADS // B_KV_HEADS
                n_pairs = B_KV_HEADS // 2
                return _attention(functools.partial(_attn_b_kernel, n_seg, B_SUB_TILE, seq), q, ks, vs,
                                  [b_sink[r] * LOG2E],
                                  [pl.BlockSpec(memory_space=pltpu.SMEM)], n_pairs,
                                  n_pairs if all_groups else 1,
                                  group * LANES, LANES, LANES, group * LANES, tq, "attn_b")
            if kind == 2:
                return _attention(functools.partial(_attn_plain_kernel, n_seg, n_units, 1, 2 * LANES, C_V_DIM),
                                  q, ks, vs, [], [], C_HEADS, C_HEADS if all_groups else 1,
                                  2 * LANES, 2 * LANES, C_V_DIM, C_V_DIM, tq, "attn_c")
            group = D_HEADS // D_KV_HEADS
            return _attention(functools.partial(_attn_plain_kernel, n_seg, n_units, group, D_HEAD_DIM,
                                                D_HEAD_DIM),
                              q, ks, vs, [], [], D_KV_HEADS, D_KV_HEADS if all_groups else 1,
                              group * D_HEAD_DIM, D_HEAD_DIM, D_HEAD_DIM, group * D_HEAD_DIM, tq, "attn_d")

        if kind == 0:
            w_o = a_w_o[r]
        elif kind == 1:
            w_o = b_w_o[r].reshape(B_KV_HEADS // 2, 2, B_HEADS // B_KV_HEADS, B_HEAD_DIM, d)
            w_o = jnp.swapaxes(w_o, 1, 2).reshape(-1, d)
        elif kind == 2:
            w_o = c_w_o[r]
        else:
            w_o = d_w_o[r]
        w_o = w_o.astype(BF16)
        ffn_w = _ffn_weights(ffn_up[i], ffn_conv_w[i], ffn_conv_b[i], ffn_down[i])

        tq = min(tq_lat[kind], seq)
        ol = attend(ql, [kc, kl], [vc, vl], tq, 4 if tq >= 1024 else 1, False).reshape(batch * seq, -1)
        xl = _block_tail(xl, ol, mod, lat_row, norm2_g[i], w_o, *ffn_w, tm_lat, seq,
                         final_g=None if need_ctx else final_g)
        if need_ctx:
            oc = attend(qc, [kc], [vc], tq_ctx, 1, True).reshape(batch * n_ctx, -1)
            xc = _block_tail(xc, oc, mod, ctx_row, norm2_g[i], w_o, *ffn_w, tm_ctx, n_ctx)

    return xl.reshape(batch, seq, d)
```

```python
import functools
import math

import jax
import jax.numpy as jnp
from jax import lax
from jax.experimental import pallas as pl
from jax.experimental.pallas import tpu as pltpu

F32 = jnp.float32
BF16 = jnp.bfloat16

GRID_W = 64
ROPE_THETA = 10000.0
NORM_EPS = 1e-6
LOG2E = math.log2(math.e)

A_HEADS, A_HEAD_DIM = 8, 64
B_HEADS, B_KV_HEADS, B_HEAD_DIM, B_WINDOW = 16, 4, 64, 128
C_HEADS, C_Q_RANK, C_KV_RANK, C_NOPE_DIM, C_ROPE_DIM, C_V_DIM = 8, 384, 256, 128, 64, 128
D_HEADS, D_KV_HEADS, D_HEAD_DIM = 8, 4, 128
CONV_W = 3

LANES = 128
SUBLANES = 8
MOD_ROWS = 16
VMEM_LIMIT = 56 * 1024 * 1024
FF_CHUNK = 256
HALO = SUBLANES
O_HALO = 2 * SUBLANES
KEY_SEG = 512
B_SUB_TILE = 128
PROJ_SUB_ROWS = {0: 512, 1: 512, 2: 512, 3: 256}


def _params(n_axes):
    return pltpu.CompilerParams(dimension_semantics=("arbitrary",) * n_axes,
                                vmem_limit_bytes=VMEM_LIMIT)


def _dot(a, b):
    return jnp.dot(a, b, preferred_element_type=F32)


def _dot_nt(a, b):
    return lax.dot_general(a, b, (((1,), (1,)), ((), ())), preferred_element_type=F32)


def _lane_iota(n=LANES):
    return lax.broadcasted_iota(jnp.int32, (1, n), 1)


def _ada_kernel(c_ref, w_ref, b_ref, o_ref):
    cv = c_ref[...]
    s = cv / (1.0 + jnp.exp(-cv))
    o_ref[0] = _dot(s.astype(BF16), w_ref[0].astype(BF16)) + b_ref[0]


def _ada_mods(c, c_ctx, ada_w, ada_b):
    n_layers, d, n_out = ada_w.shape
    batch = c.shape[0]
    rows = jnp.zeros((MOD_ROWS, d), F32).at[:batch].set(c).at[batch].set(c_ctx)
    tn = 1536
    out = pl.pallas_call(
        _ada_kernel,
        grid=(n_layers, n_out // tn),
        in_specs=[pl.BlockSpec((MOD_ROWS, d), lambda l, j: (0, 0)),
                  pl.BlockSpec((1, d, tn), lambda l, j: (l, 0, j)),
                  pl.BlockSpec((1, 1, tn), lambda l, j: (l, 0, j))],
        out_specs=pl.BlockSpec((1, MOD_ROWS, tn), lambda l, j: (l, 0, j)),
        out_shape=jax.ShapeDtypeStruct((n_layers, MOD_ROWS, n_out), F32),
        compiler_params=_params(2),
        name="ada_mods",
    )(rows, ada_w, ada_b.reshape(n_layers, 1, n_out))
    return out.reshape(n_layers, MOD_ROWS, 6, d)


def _modulated(x, mod, g, shift_row, scale_row):
    ms = jnp.mean(x * x, axis=-1, keepdims=True)
    y = x * lax.rsqrt(ms + NORM_EPS)
    return (y * g) * (1.0 + mod[scale_row:scale_row + 1]) + mod[shift_row:shift_row + 1]


def _rms(x, g):
    ms = jnp.mean(x * x, axis=-1, keepdims=True)
    return x * lax.rsqrt(ms + NORM_EPS) * g


def _rope64(xc, cos, sin, lo32):
    r_dn = pltpu.roll(xc, 32, 1)
    r_up = pltpu.roll(xc, 96, 1)
    return xc * cos + jnp.where(lo32, r_up, r_dn) * sin


def _rope128(xc, cos, sin):
    return xc * cos + pltpu.roll(xc, 64, 1) * sin


def _rope_tables(n_tokens, rot_dim):
    n_rows = n_tokens // GRID_W
    rows = jnp.repeat(jnp.arange(n_rows, dtype=F32), GRID_W)
    cols = jnp.tile(jnp.arange(GRID_W, dtype=F32), n_rows)
    n_freq = rot_dim // 4
    inv_freq = ROPE_THETA ** (-jnp.arange(n_freq, dtype=F32) / n_freq)
    ang = jnp.concatenate([rows[:, None] * inv_freq, cols[:, None] * inv_freq], axis=-1)
    cos, sin = jnp.cos(ang), jnp.sin(ang)
    reps = LANES // rot_dim
    return (jnp.tile(jnp.concatenate([cos, cos], axis=-1), (1, reps)),
            jnp.tile(jnp.concatenate([-sin, sin], axis=-1), (1, reps)))


def _lane_blocks(parts):
    return [s[:, j * LANES:(j + 1) * LANES] for s in parts for j in range(s.shape[1] // LANES)]


def _softmax_pv(scores, values, sink=None):
    m = jnp.max(functools.reduce(jnp.maximum, _lane_blocks(scores)), axis=-1, keepdims=True)
    if sink is not None:
        r = m.shape[0] // len(sink)
        m = jnp.concatenate([jnp.maximum(m[j * r:(j + 1) * r], sj) for j, sj in enumerate(sink)], axis=0)
    o = functools.reduce(jnp.add, [_dot(jnp.exp2(s - m).astype(BF16), v) for s, v in zip(scores, values)])
    l = o[:, LANES:]
    if sink is not None:
        l = l + jnp.concatenate([jnp.exp2(sj - m[j * r:(j + 1) * r]) for j, sj in enumerate(sink)], axis=0)
    return o[:, :LANES] / l


def _add_row_tiled(s, bias):
    r = bias.shape[0]
    return jnp.concatenate([s[j * r:(j + 1) * r] + bias for j in range(s.shape[0] // r)], axis=0)


def _with_ones(values):
    assert all(v.shape[1] == LANES for v in values)
    return [jnp.concatenate([v, jnp.ones_like(v)], axis=1) for v in values]


def _attend_units(units):
    all_scores = []
    for q, keys, _, biases, _ in units:
        scores = [_dot_nt(q, k) for k in keys]
        if biases is not None:
            scores = [s if b is None else _add_row_tiled(s, b) for s, b in zip(scores, biases)]
        all_scores.append(scores)
    return [_softmax_pv(scores, unit[2], unit[4]) for scores, unit in zip(all_scores, units)]


def _segments(refs, seg, lanes):
    out = []
    for r in refs:
        n = r.shape[1]
        if n <= seg:
            out.append(r[0, :, lanes])
        else:
            out += [r[0, j * seg:(j + 1) * seg, lanes] for j in range(n // seg)]
    return out


def _proj_a_kernel(rope, *refs):
    if rope:
        x_ref, mod_ref, g_ref, cos_ref, sin_ref, w_ref, q_ref, k_ref, v_ref = refs
    else:
        x_ref, mod_ref, g_ref, w_ref, q_ref, k_ref, v_ref = refs
    h = _modulated(x_ref[...], mod_ref[0], g_ref[...], 0, 1).astype(BF16)
    qkv = _dot(h, w_ref[...])
    width = q_ref.shape[1]
    q_scale = A_HEAD_DIM ** -0.5 * LOG2E
    lo32 = (_lane_iota() % 64) < 32
    for c in range(width // LANES):
        sl = slice(c * LANES, (c + 1) * LANES)
        qc = qkv[:, sl]
        kc = qkv[:, width + c * LANES: width + (c + 1) * LANES]
        if rope:
            qc = _rope64(qc, cos_ref[...], sin_ref[...], lo32)
            kc = _rope64(kc, cos_ref[...], sin_ref[...], lo32)
        q_ref[:, sl] = (qc * q_scale).astype(BF16)
        k_ref[:, sl] = kc.astype(BF16)
    v_ref[...] = qkv[:, 2 * width:].astype(BF16)


def _proj_b_kernel(rope, *refs):
    if rope:
        x_ref, mod_ref, g_ref, cos_ref, sin_ref, w_ref, q_ref, k_ref, v_ref = refs
    else:
        x_ref, mod_ref, g_ref, w_ref, q_ref, k_ref, v_ref = refs
    h = _modulated(x_ref[...], mod_ref[0], g_ref[...], 0, 1).astype(BF16)
    qkv = _dot(h, w_ref[...])
    nq, nkv = q_ref.shape[1], k_ref.shape[1]
    q_scale = B_HEAD_DIM ** -0.5 * LOG2E
    lo32 = (_lane_iota() % 64) < 32
    for c in range(nq // LANES):
        sl = slice(c * LANES, (c + 1) * LANES)
        qc = qkv[:, sl]
        if rope:
            qc = _rope64(qc, cos_ref[...], sin_ref[...], lo32)
        q_ref[:, sl] = (qc * q_scale).astype(BF16)
    for c in range(nkv // LANES):
        sl = slice(c * LANES, (c + 1) * LANES)
        kc = qkv[:, nq + c * LANES: nq + (c + 1) * LANES]
        if rope:
            kc = _rope64(kc, cos_ref[...], sin_ref[...], lo32)
        k_ref[:, sl] = kc.astype(BF16)
    v_ref[...] = qkv[:, nq + nkv:].astype(BF16)


def _proj_c_kernel(rope, *refs):
    if rope:
        (x_ref, mod_ref, g_ref, cos_ref, sin_ref, wd_ref, qg_ref, kvg_ref, wuq_ref, wukv_ref,
         q_ref, k_ref, v_ref) = refs
    else:
        x_ref, mod_ref, g_ref, wd_ref, qg_ref, kvg_ref, wuq_ref, wukv_ref, q_ref, k_ref, v_ref = refs
    h = _modulated(x_ref[...], mod_ref[0], g_ref[...], 0, 1).astype(BF16)
    down = _dot(h, wd_ref[...])
    cq = _rms(down[:, :C_Q_RANK], qg_ref[...]).astype(BF16)
    ckv = _rms(down[:, C_Q_RANK:C_Q_RANK + C_KV_RANK], kvg_ref[...]).astype(BF16)
    k_rope = down[:, C_Q_RANK + C_KV_RANK:]
    q = _dot(cq, wuq_ref[...])
    kv = _dot(ckv, wukv_ref[...])
    q_scale = (C_NOPE_DIM + C_ROPE_DIM) ** -0.5 * LOG2E
    lo32 = (_lane_iota() % 64) < 32
    if rope:
        k_rope = _rope64(k_rope, cos_ref[...], sin_ref[...], lo32)
    k_rope = k_rope.astype(BF16)
    for hd in range(C_HEADS):
        lo = slice(2 * hd * LANES, (2 * hd + 1) * LANES)
        hi = slice((2 * hd + 1) * LANES, (2 * hd + 2) * LANES)
        q_rope = q[:, hi]
        if rope:
            q_rope = _rope64(q_rope, cos_ref[...], sin_ref[...], lo32)
        q_ref[:, lo] = (q[:, lo] * q_scale).astype(BF16)
        q_ref[:, hi] = (q_rope * q_scale).astype(BF16)
        k_ref[:, lo] = kv[:, lo].astype(BF16)
        k_ref[:, hi] = k_rope
        v_ref[:, hd * LANES:(hd + 1) * LANES] = kv[:, hi].astype(BF16)


def _proj_d_kernel(rope, *refs):
    if rope:
        x_ref, mod_ref, g_ref, cos_ref, sin_ref, w_ref, qg_ref, kg_ref, q_ref, k_ref, v_ref = refs
    else:
        x_ref, mod_ref, g_ref, w_ref, qg_ref, kg_ref, q_ref, k_ref, v_ref = refs
    h = _modulated(x_ref[...], mod_ref[0], g_ref[...], 0, 1).astype(BF16)
    qkv = _dot(h, w_ref[...])
    nq, nkv = q_ref.shape[1], k_ref.shape[1]
    q_scale = D_HEAD_DIM ** -0.5 * LOG2E
    for c in range(nq // LANES):
        sl = slice(c * LANES, (c + 1) * LANES)
        qc = _rms(qkv[:, sl], qg_ref[...])
        if rope:
            qc = _rope128(qc, cos_ref[...], sin_ref[...])
        q_ref[:, sl] = (qc * q_scale).astype(BF16)
    for c in range(nkv // LANES):
        sl = slice(c * LANES, (c + 1) * LANES)
        kc = _rms(qkv[:, nq + c * LANES: nq + (c + 1) * LANES], kg_ref[...])
        if rope:
            kc = _rope128(kc, cos_ref[...], sin_ref[...])
        k_ref[:, sl] = kc.astype(BF16)
    v_ref[...] = qkv[:, nq + nkv:].astype(BF16)


def _row_sub_blocks(body, rope, n_out, sub_rows, *refs):
    tm = refs[0].shape[0]
    sub = min(tm, sub_rows)
    n_in = len(refs) - n_out
    per_row = {0} | ({3, 4} if rope else set()) | set(range(n_in, len(refs)))
    for r0 in range(0, tm, sub):
        body(rope, *[r.at[pl.ds(r0, sub)] if i in per_row else r for i, r in enumerate(refs)])


def _project(body, x, mod, mod_row, g, rope_tabs, weights, out_widths, tm, seq, sub_rows):
    rows, d = x.shape
    tiles_per_seq = seq // tm
    const = lambda i: (0, 0)
    in_specs = [pl.BlockSpec((tm, d), lambda i: (i, 0)),
                pl.BlockSpec((1, 6, d), lambda i: (mod_row(i), 0, 0)),
                pl.BlockSpec((1, d), const)]
    args = [x, mod, g.reshape(1, d)]
    if rope_tabs is not None:
        in_specs += [pl.BlockSpec((tm, LANES), lambda i: (i % tiles_per_seq, 0))] * 2
        args += list(rope_tabs)
    for w in weights:
        in_specs.append(pl.BlockSpec(w.shape, const))
        args.append(w)
    return pl.pallas_call(
        functools.partial(_row_sub_blocks, body, rope_tabs is not None, len(out_widths), sub_rows),
        grid=(rows // tm,),
        in_specs=in_specs,
        out_specs=[pl.BlockSpec((tm, w), lambda i: (i, 0)) for w in out_widths],
        out_shape=[jax.ShapeDtypeStruct((rows, w), BF16) for w in out_widths],
        compiler_params=_params(1),
        name=body.__name__.strip("_"),
    )(*args)


def _attn_a_kernel(n_seg, n_units, lambda_init, *refs, gps):
    q_ref = refs[0]
    k_refs = refs[1:1 + n_seg]
    v_refs = refs[1 + n_seg:1 + 2 * n_seg]
    lq1_ref, lk1_ref, lq2_ref, lk2_ref, g_ref, o_ref = refs[1 + 2 * n_seg:]
    lam = (jnp.exp(jnp.sum(lq1_ref[...] * lk1_ref[...], keepdims=True))
           - jnp.exp(jnp.sum(lq2_ref[...] * lk2_ref[...], keepdims=True)) + lambda_init)
    tq = q_ref.shape[1] // n_units
    lane = _lane_iota()
    zero = jnp.zeros((tq, LANES), BF16)
    units = []
    for g in range(gps):
        lanes = slice(g * LANES, (g + 1) * LANES)
        keys, values = _segments(k_refs, KEY_SEG, lanes), _with_ones(_segments(v_refs, KEY_SEG, lanes))
        for u in range(n_units):
            q = q_ref[0, u * tq:(u + 1) * tq, lanes]
            qs = jnp.concatenate([jnp.where(lane < 64, q, zero), jnp.where(lane >= 64, q, zero)], axis=0)
            units.append((qs, keys, values, None, None))
    outs = _attend_units(units)
    for i, o in enumerate(outs):
        g, u = divmod(i, n_units)
        diff = o[:tq] - lam * o[tq:]
        o_ref[0, u * tq:(u + 1) * tq, g * LANES:(g + 1) * LANES] = (
            _rms(diff, g_ref[...]) * (1.0 - lambda_init)).astype(BF16)


def _attn_b_kernel(n_seg, tq, n_lat, *refs, gps):
    q_ref = refs[0]
    k_refs = refs[1:1 + n_seg]
    v_refs = refs[1 + n_seg:1 + 2 * n_seg]
    sink_ref, o_ref = refs[1 + 2 * n_seg:]
    lane = _lane_iota()
    group = B_HEADS // B_KV_HEADS
    n_sub = q_ref.shape[1] // tq
    units = []
    for g in range(gps):
        pair = pl.program_id(1) * gps + g
        lanes = slice(g * LANES, (g + 1) * LANES)
        kc, vc = k_refs[0][0, :, lanes], _with_ones([v_refs[0][0, :, lanes]])[0]
        for t in range(n_sub):
            keys, values, biases = [kc], [vc], None
            if n_seg == 2:
                span = tq + 2 * B_WINDOW
                start = (pl.program_id(2) * n_sub + t) * tq
                k0 = pl.multiple_of(jnp.clip(start - B_WINDOW, 0, n_lat - span), LANES)
                keys.append(k_refs[1][0, pl.ds(k0, span), lanes])
                values += _with_ones([v_refs[1][0, pl.ds(k0, span), lanes]])
                q_pos = start + lax.broadcasted_iota(jnp.int32, (tq, 1), 0)
                k_pos = k0 + lax.broadcasted_iota(jnp.int32, (1, span), 1)
                biases = [None, jnp.where(jnp.abs(q_pos - k_pos) <= B_WINDOW, 0.0, -jnp.inf)]
            rows = slice(t * tq, (t + 1) * tq)
            for half in range(2):
                keep = (lane >= 64) if half else (lane < 64)
                q = jnp.concatenate(
                    [jnp.where(keep, q_ref[0, rows, (g * group + c) * LANES:(g * group + c + 1) * LANES],
                               jnp.zeros((tq, LANES), BF16)) for c in range(group)], axis=0)
                sink = [sink_ref[(2 * pair + half) * group + c] for c in range(group)]
                units.append((q, keys, values, biases, sink))
    outs = _attend_units(units)
    for g in range(gps):
        for t in range(n_sub):
            lo, hi = outs[2 * (g * n_sub + t)], outs[2 * (g * n_sub + t) + 1]
            for c in range(group):
                blk = slice(c * tq, (c + 1) * tq)
                o_ref[0, t * tq:(t + 1) * tq, (g * group + c) * LANES:(g * group + c + 1) * LANES] = (
                    jnp.where(lane < 64, lo[blk], hi[blk]).astype(BF16))


def _attn_plain_kernel(n_seg, n_units, n_heads, dq, dv, *refs, gps):
    q_ref = refs[0]
    k_refs = refs[1:1 + n_seg]
    v_refs = refs[1 + n_seg:1 + 2 * n_seg]
    o_ref = refs[1 + 2 * n_seg]
    tq = q_ref.shape[1] // n_units
    units = []
    for g in range(gps):
        keys = _segments(k_refs, KEY_SEG, slice(g * dq, (g + 1) * dq))
        values = _with_ones(_segments(v_refs, KEY_SEG, slice(g * dv, (g + 1) * dv)))
        for u in range(n_units):
            rows = slice(u * tq, (u + 1) * tq)
            heads = [q_ref[0, rows, (g * n_heads + hd) * dq:(g * n_heads + hd + 1) * dq]
                     for hd in range(n_heads)]
            units.append((heads[0] if n_heads == 1 else jnp.concatenate(heads, axis=0),
                          keys, values, None, None))
    outs = _attend_units(units)
    for i, o in enumerate(outs):
        g, u = divmod(i, n_units)
        for hd in range(n_heads):
            o_ref[0, u * tq:(u + 1) * tq, (g * n_heads + hd) * dv:(g * n_heads + hd + 1) * dv] = (
                o[hd * tq:(hd + 1) * tq].astype(BF16))


def _attention(body, q, ks, vs, extra, extra_specs, n_groups, gps, wq, wk, wv, wo, tq, name):
    b, s, _ = q.shape
    wq, wk, wv, wo = gps * wq, gps * wk, gps * wv, gps * wo
    in_specs = [pl.BlockSpec((1, tq, wq), lambda bi, gi, qi: (bi, qi, gi))]
    for k in ks:
        in_specs.append(pl.BlockSpec((1, k.shape[1], wk), lambda bi, gi, qi: (bi, 0, gi)))
    for v in vs:
        in_specs.append(pl.BlockSpec((1, v.shape[1], wv), lambda bi, gi, qi: (bi, 0, gi)))
    return pl.pallas_call(
        functools.partial(body, gps=gps),
        grid=(b, n_groups // gps, s // tq),
        in_specs=in_specs + extra_specs,
        out_specs=pl.BlockSpec((1, tq, wo), lambda bi, gi, qi: (bi, qi, gi)),
        out_shape=jax.ShapeDtypeStruct((b, s, (n_groups // gps) * wo), BF16),
        compiler_params=_params(3),
        name=name,
    )(q, *ks, *vs, *extra)


def _small_spec(a):
    return pl.BlockSpec(a.shape, lambda bi, gi, qi: (0,) * a.ndim)


def _block_tail_kernel(tiles_per_seq, final, *refs):
    (x_ref, xp_ref, xn_ref, o_ref, op_ref, on_ref, mod_ref, g_ref, wo_ref, wup_ref, cw_ref, cb_ref,
     wdn_ref) = refs[:13]
    fg_ref = refs[13] if final else None
    out_ref, act_ref = refs[-2:]
    tm, d = x_ref.shape
    n_ext = tm + 2 * HALO
    pos = pl.program_id(0) % tiles_per_seq
    mod = mod_ref[0]
    x_ext = jnp.concatenate([xp_ref[...], x_ref[...], xn_ref[...]], axis=0)
    o_ext = jnp.concatenate([op_ref[...], o_ref[...], on_ref[...]], axis=0)
    mixed = _dot(o_ext, wo_ref[...])[O_HALO - HALO:O_HALO - HALO + n_ext]
    x1_ext = x_ext + mod[2:3] * mixed
    row = lax.broadcasted_iota(jnp.int32, (n_ext, 1), 0)
    first_kept = jnp.where(pos > 0, 0, HALO)
    end_kept = jnp.where(pos < tiles_per_seq - 1, n_ext, HALO + tm)
    keep = (row >= first_kept) & (row < end_kept)
    h_ext = jnp.where(keep, _modulated(x1_ext, mod, g_ref[...], 3, 4), 0.0).astype(BF16)
    ff = wdn_ref.shape[0]

    def conv_branch(cols):
        u = _dot(h_ext, wup_ref[:, cols])
        u_prev = pltpu.roll(u, 1, 0)[HALO:HALO + tm]
        u_next = pltpu.roll(u, n_ext - 1, 0)[HALO:HALO + tm]
        return (cb_ref[:, cols] + u_prev * cw_ref[0:1, cols] + u[HALO:HALO + tm] * cw_ref[1:2, cols]
                + u_next * cw_ref[2:3, cols])

    for c in range(ff // FF_CHUNK):
        a = conv_branch(slice(c * FF_CHUNK, (c + 1) * FF_CHUNK))
        gate = conv_branch(slice(ff + c * FF_CHUNK, ff + (c + 1) * FF_CHUNK))
        act_ref[:, c * FF_CHUNK:(c + 1) * FF_CHUNK] = ((a / (1.0 + jnp.exp(-a))) * gate).astype(BF16)
    out = x1_ext[HALO:HALO + tm] + mod[5:6] * _dot(act_ref[...], wdn_ref[...])
    if final:
        out = _rms(out, fg_ref[...])
    out_ref[...] = out


def _block_tail(x, o, mod, mod_row, g, w_o, w_up, conv_w, conv_b, w_down, tm, seq, final_g=None):
    rows, d = x.shape
    tiles_per_seq = seq // tm
    const2 = lambda i: (0, 0)
    once = pl.Buffered(1)

    def halo_specs(width, n_halo):
        per_tile = tm // n_halo
        last = rows // n_halo - 1
        return [pl.BlockSpec((tm, width), lambda i: (i, 0)),
                pl.BlockSpec((n_halo, width), lambda i: (jnp.maximum(i * per_tile - 1, 0), 0)),
                pl.BlockSpec((n_halo, width), lambda i: (jnp.minimum((i + 1) * per_tile, last), 0))]

    in_specs = (halo_specs(d, HALO) + halo_specs(o.shape[1], O_HALO) +
                [pl.BlockSpec((1, 6, d), lambda i: (mod_row(i), 0, 0)),
                 pl.BlockSpec((1, d), const2),
                 pl.BlockSpec(w_o.shape, const2, pipeline_mode=once),
                 pl.BlockSpec(w_up.shape, const2, pipeline_mode=once),
                 pl.BlockSpec(conv_w.shape, const2),
                 pl.BlockSpec(conv_b.shape, const2),
                 pl.BlockSpec(w_down.shape, const2, pipeline_mode=once)])
    args = [x, x, x, o, o, o, mod, g.reshape(1, d), w_o, w_up, conv_w, conv_b, w_down]
    if final_g is not None:
        in_specs.append(pl.BlockSpec((1, d), const2))
        args.append(final_g.reshape(1, d))
    return pl.pallas_call(
        functools.partial(_block_tail_kernel, tiles_per_seq, final_g is not None),
        grid=(rows // tm,),
        in_specs=in_specs,
        out_specs=pl.BlockSpec((tm, d), lambda i: (i, 0)),
        out_shape=jax.ShapeDtypeStruct((rows, d), F32),
        scratch_shapes=[pltpu.VMEM((tm, w_down.shape[0]), BF16)],
        compiler_params=_params(1),
        name="block_tail",
    )(*args)


def _ffn_weights(w_up, conv_w, conv_b, w_down):
    return w_up.astype(BF16), conv_w, conv_b.reshape(1, -1), w_down.astype(BF16)


def kernel(x, c, ctx, c_ctx, ada_w, ada_b, norm1_g, norm2_g, ffn_up, ffn_conv_w, ffn_conv_b, ffn_down,
           a_w_qkv, a_w_o, a_lambda_q1, a_lambda_k1, a_lambda_q2, a_lambda_k2, a_subln_g,
           b_w_qkv, b_w_o, b_sink,
           c_w_down, c_q_norm_g, c_kv_norm_g, c_w_uq, c_w_ukv, c_w_o,
           d_w_qkv, d_q_norm_g, d_k_norm_g, d_w_o,
           final_g):
    batch, seq, d = x.shape
    n_ctx = ctx.shape[1]
    depth = ada_w.shape[0]
    n_mixers = 4

    mods = _ada_mods(c, c_ctx, ada_w, ada_b)
    rope64 = _rope_tables(seq, 64)
    rope128 = _rope_tables(seq, 128)

    tm_lat = min(512, seq)
    tm_proj = min(1024, seq)
    tm_ctx = n_ctx
    tq_lat = {0: 1024, 1: 512, 2: 2048, 3: 1024}
    tq_ctx = n_ctx
    lat_row = lambda i: i // (seq // tm_lat)
    proj_row = lambda i: i // (seq // tm_proj)
    ctx_row = lambda i: batch

    xl = x.reshape(batch * seq, d)
    xc = ctx.reshape(batch * n_ctx, d)

    def to3(t, n):
        return t.reshape(batch, n, t.shape[-1])

    for i in range(depth):
        kind, r = i % n_mixers, i // n_mixers
        need_ctx = i < depth - 1
        mod = mods[i]
        g1 = norm1_g[i]

        if kind == 0:
            width = 2 * A_HEADS * A_HEAD_DIM
            weights = [a_w_qkv[r].astype(BF16)]
            widths = [width, width, width]
            body, tabs = _proj_a_kernel, rope64
        elif kind == 1:
            nq, nkv = B_HEADS * B_HEAD_DIM, B_KV_HEADS * B_HEAD_DIM
            group = B_HEADS // B_KV_HEADS
            w = b_w_qkv[r]
            w_q = w[:, :nq].reshape(d, B_KV_HEADS // 2, 2, group, B_HEAD_DIM)
            w_q = jnp.swapaxes(w_q, 2, 3).reshape(d, nq)
            weights = [jnp.concatenate([w_q, w[:, nq:]], axis=1).astype(BF16)]
            widths = [nq, nkv, nkv]
            body, tabs = _proj_b_kernel, rope64
        elif kind == 2:
            wd = jnp.pad(c_w_down[r], ((0, 0), (0, LANES - C_ROPE_DIM)))
            wuq = c_w_uq[r].reshape(C_Q_RANK, C_HEADS, C_NOPE_DIM + C_ROPE_DIM)
            wuq = jnp.pad(wuq, ((0, 0), (0, 0), (0, 2 * LANES - C_NOPE_DIM - C_ROPE_DIM)))
            weights = [wd.astype(BF16), c_q_norm_g[r].reshape(1, -1), c_kv_norm_g[r].reshape(1, -1),
                       wuq.reshape(C_Q_RANK, C_HEADS * 2 * LANES).astype(BF16), c_w_ukv[r].astype(BF16)]
            widths = [C_HEADS * 2 * LANES, C_HEADS * 2 * LANES, C_HEADS * C_V_DIM]
            body, tabs = _proj_c_kernel, rope64
        else:
            weights = [d_w_qkv[r].astype(BF16), d_q_norm_g[r].reshape(1, -1), d_k_norm_g[r].reshape(1, -1)]
            widths = [D_HEADS * D_HEAD_DIM, D_KV_HEADS * D_HEAD_DIM, D_KV_HEADS * D_HEAD_DIM]
            body, tabs = _proj_d_kernel, rope128

        ql, kl, vl = _project(body, xl, mod, proj_row, g1, tabs, weights, widths, tm_proj, seq,
                              PROJ_SUB_ROWS[kind])
        qc, kc, vc = _project(body, xc, mod, ctx_row, g1, None, weights, widths, tm_ctx, n_ctx,
                              PROJ_SUB_ROWS[kind])
        ql, kl, vl = to3(ql, seq), to3(kl, seq), to3(vl, seq)
        qc, kc, vc = to3(qc, n_ctx), to3(kc, n_ctx), to3(vc, n_ctx)

        def attend(q, ks, vs, tq, n_units, all_groups):
            n_seg = len(ks)
            if kind == 0:
                lambda_init = 0.8 - 0.6 * math.exp(-0.3 * i)
                extra = [a_lambda_q1[r].reshape(1, -1), a_lambda_k1[r].reshape(1, -1),
                         a_lambda_q2[r].reshape(1, -1), a_lambda_k2[r].reshape(1, -1),
                         a_subln_g[r].reshape(1, -1)]
                return _attention(functools.partial(_attn_a_kernel, n_seg, n_units, lambda_init), q, ks, vs,
                                  extra, [_small_spec(e) for e in extra], A_HEADS,
                                  A_HEADS if all_groups else 1, LANES, LANES, LANES, LANES, tq, "attn_a")
            if kind == 1:
                group = B_HEADS // B_KV_HEADS
                n_pairs = B_KV_HEADS // 2
                return _attention(functools.partial(_attn_b_kernel, n_seg, B_SUB_TILE, seq), q, ks, vs,
                                  [b_sink[r] * LOG2E],
                                  [pl.BlockSpec(memory_space=pltpu.SMEM)], n_pairs,
                                  n_pairs if all_groups else 1,
                                  group * LANES, LANES, LANES, group * LANES, tq, "attn_b")
            if kind == 2:
                return _attention(functools.partial(_attn_plain_kernel, n_seg, n_units, 1, 2 * LANES, C_V_DIM),
                                  q, ks, vs, [], [], C_HEADS, C_HEADS if all_groups else 1,
                                  2 * LANES, 2 * LANES, C_V_DIM, C_V_DIM, tq, "attn_c")
            group = D_HEADS // D_KV_HEADS
            return _attention(functools.partial(_attn_plain_kernel, n_seg, n_units, group, D_HEAD_DIM,
                                                D_HEAD_DIM),
                              q, ks, vs, [], [], D_KV_HEADS, D_KV_HEADS if all_groups else 1,
                              group * D_HEAD_DIM, D_HEAD_DIM, D_HEAD_DIM, group * D_HEAD_DIM, tq, "attn_d")

        if kind == 0:
            w_o = a_w_o[r]
        elif kind == 1:
            w_o = b_w_o[r].reshape(B_KV_HEADS // 2, 2, B_HEADS // B_KV_HEADS, B_HEAD_DIM, d)
            w_o = jnp.swapaxes(w_o, 1, 2).reshape(-1, d)
        elif kind == 2:
            w_o = c_w_o[r]
        else:
            w_o = d_w_o[r]
        w_o = w_o.astype(BF16)
        ffn_w = _ffn_weights(ffn_up[i], ffn_conv_w[i], ffn_conv_b[i], ffn_down[i])

        tq = min(tq_lat[kind], seq)
        ol = attend(ql, [kc, kl], [vc, vl], tq, 4 if tq >= 1024 else 1, False).reshape(batch * seq, -1)
        xl = _block_tail(xl, ol, mod, lat_row, norm2_g[i], w_o, *ffn_w, tm_lat, seq,
                         final_g=None if need_ctx else final_g)
        if need_ctx:
            oc = attend(qc, [kc], [vc], tq_ctx, 1, True).reshape(batch * n_ctx, -1)
            xc = _block_tail(xc, oc, mod, ctx_row, norm2_g[i], w_o, *ffn_w, tm_ctx, n_ctx)

    return xl.reshape(batch, seq, d)
```

```python
import functools
import math

import jax
import jax.numpy as jnp
from jax import lax
from jax.experimental import pallas as pl
from jax.experimental.pallas import tpu as pltpu

F32 = jnp.float32
BF16 = jnp.bfloat16

GRID_W = 64
ROPE_THETA = 10000.0
NORM_EPS = 1e-6
LOG2E = math.log2(math.e)

A_HEADS, A_HEAD_DIM = 8, 64
B_HEADS, B_KV_HEADS, B_HEAD_DIM, B_WINDOW = 16, 4, 64, 128
C_HEADS, C_Q_RANK, C_KV_RANK, C_NOPE_DIM, C_ROPE_DIM, C_V_DIM = 8, 384, 256, 128, 64, 128
D_HEADS, D_KV_HEADS, D_HEAD_DIM = 8, 4, 128
CONV_W = 3

LANES = 128
SUBLANES = 8
MOD_ROWS = 16
VMEM_LIMIT = 56 * 1024 * 1024
FF_CHUNK = 256
HALO = SUBLANES
O_HALO = 2 * SUBLANES
KEY_SEG = 512
ATTN_UNITS = 4
ATTN_UNIT_ROWS = 512
B_SUB_TILE = 128
PROJ_SUB_ROWS = {0: 512, 1: 512, 2: 512, 3: 256}


def _params(n_axes):
    return pltpu.CompilerParams(dimension_semantics=("arbitrary",) * n_axes,
                                vmem_limit_bytes=VMEM_LIMIT)


def _dot(a, b):
    return jnp.dot(a, b, preferred_element_type=F32)


def _dot_nt(a, b):
    return lax.dot_general(a, b, (((1,), (1,)), ((), ())), preferred_element_type=F32)


def _lane_iota(n=LANES):
    return lax.broadcasted_iota(jnp.int32, (1, n), 1)


def _ada_kernel(c_ref, w_ref, b_ref, o_ref):
    cv = c_ref[...]
    s = cv / (1.0 + jnp.exp(-cv))
    o_ref[0] = _dot(s.astype(BF16), w_ref[0].astype(BF16)) + b_ref[0]


def _ada_mods(c, c_ctx, ada_w, ada_b):
    n_layers, d, n_out = ada_w.shape
    batch = c.shape[0]
    rows = jnp.zeros((MOD_ROWS, d), F32).at[:batch].set(c).at[batch].set(c_ctx)
    tn = 1536
    out = pl.pallas_call(
        _ada_kernel,
        grid=(n_layers, n_out // tn),
        in_specs=[pl.BlockSpec((MOD_ROWS, d), lambda l, j: (0, 0)),
                  pl.BlockSpec((1, d, tn), lambda l, j: (l, 0, j)),
                  pl.BlockSpec((1, 1, tn), lambda l, j: (l, 0, j))],
        out_specs=pl.BlockSpec((1, MOD_ROWS, tn), lambda l, j: (l, 0, j)),
        out_shape=jax.ShapeDtypeStruct((n_layers, MOD_ROWS, n_out), F32),
        compiler_params=_params(2),
        name="ada_mods",
    )(rows, ada_w, ada_b.reshape(n_layers, 1, n_out))
    return out.reshape(n_layers, MOD_ROWS, 6, d)


def _modulated(x, mod, g, shift_row, scale_row):
    ms = jnp.mean(x * x, axis=-1, keepdims=True)
    y = x * lax.rsqrt(ms + NORM_EPS)
    return (y * g) * (1.0 + mod[scale_row:scale_row + 1]) + mod[shift_row:shift_row + 1]


def _rms(x, g):
    ms = jnp.mean(x * x, axis=-1, keepdims=True)
    return x * lax.rsqrt(ms + NORM_EPS) * g


def _rope64(xc, cos, sin, lo32):
    r_dn = pltpu.roll(xc, 32, 1)
    r_up = pltpu.roll(xc, 96, 1)
    return xc * cos + jnp.where(lo32, r_up, r_dn) * sin


def _rope128(xc, cos, sin):
    return xc * cos + pltpu.roll(xc, 64, 1) * sin


def _rope_tables(n_tokens, rot_dim):
    n_rows = n_tokens // GRID_W
    rows = jnp.repeat(jnp.arange(n_rows, dtype=F32), GRID_W)
    cols = jnp.tile(jnp.arange(GRID_W, dtype=F32), n_rows)
    n_freq = rot_dim // 4
    inv_freq = ROPE_THETA ** (-jnp.arange(n_freq, dtype=F32) / n_freq)
    ang = jnp.concatenate([rows[:, None] * inv_freq, cols[:, None] * inv_freq], axis=-1)
    cos, sin = jnp.cos(ang), jnp.sin(ang)
    reps = LANES // rot_dim
    return (jnp.tile(jnp.concatenate([cos, cos], axis=-1), (1, reps)),
            jnp.tile(jnp.concatenate([-sin, sin], axis=-1), (1, reps)))


def _lane_blocks(parts):
    return [s[:, j * LANES:(j + 1) * LANES] for s in parts for j in range(s.shape[1] // LANES)]


def _softmax_pv(scores, values, sink=None):
    m = jnp.max(functools.reduce(jnp.maximum, _lane_blocks(scores)), axis=-1, keepdims=True)
    if sink is not None:
        r = m.shape[0] // len(sink)
        m = jnp.concatenate([jnp.maximum(m[j * r:(j + 1) * r], sj) for j, sj in enumerate(sink)], axis=0)
    o = functools.reduce(jnp.add, [_dot(jnp.exp2(s - m).astype(BF16), v) for s, v in zip(scores, values)])
    l = o[:, LANES:]
    if sink is not None:
        l = l + jnp.concatenate([jnp.exp2(sj - m[j * r:(j + 1) * r]) for j, sj in enumerate(sink)], axis=0)
    return o[:, :LANES] / l


def _add_row_tiled(s, bias):
    r = bias.shape[0]
    return jnp.concatenate([s[j * r:(j + 1) * r] + bias for j in range(s.shape[0] // r)], axis=0)


def _with_ones(values):
    assert all(v.shape[1] == LANES for v in values)
    return [jnp.concatenate([v, jnp.ones_like(v)], axis=1) for v in values]


def _attend_units(units):
    all_scores = []
    for q, keys, _, biases, _ in units:
        scores = [_dot_nt(q, k) for k in keys]
        if biases is not None:
            scores = [s if b is None else _add_row_tiled(s, b) for s, b in zip(scores, biases)]
        all_scores.append(scores)
    return [_softmax_pv(scores, unit[2], unit[4]) for scores, unit in zip(all_scores, units)]


def _segments(refs, seg, lanes):
    out = []
    for r in refs:
        n = r.shape[1]
        if n <= seg:
            out.append(r[0, :, lanes])
        else:
            out += [r[0, j * seg:(j + 1) * seg, lanes] for j in range(n // seg)]
    return out


def _proj_a_kernel(rope, *refs):
    if rope:
        x_ref, mod_ref, g_ref, cos_ref, sin_ref, w_ref, q_ref, k_ref, v_ref = refs
    else:
        x_ref, mod_ref, g_ref, w_ref, q_ref, k_ref, v_ref = refs
    h = _modulated(x_ref[...], mod_ref[0], g_ref[...], 0, 1).astype(BF16)
    qkv = _dot(h, w_ref[...])
    width = q_ref.shape[1]
    q_scale = A_HEAD_DIM ** -0.5 * LOG2E
    lo32 = (_lane_iota() % 64) < 32
    for c in range(width // LANES):
        sl = slice(c * LANES, (c + 1) * LANES)
        qc = qkv[:, sl]
        kc = qkv[:, width + c * LANES: width + (c + 1) * LANES]
        if rope:
            qc = _rope64(qc, cos_ref[...], sin_ref[...], lo32)
            kc = _rope64(kc, cos_ref[...], sin_ref[...], lo32)
        q_ref[:, sl] = (qc * q_scale).astype(BF16)
        k_ref[:, sl] = kc.astype(BF16)
    v_ref[...] = qkv[:, 2 * width:].astype(BF16)


def _proj_b_kernel(rope, *refs):
    if rope:
        x_ref, mod_ref, g_ref, cos_ref, sin_ref, w_ref, q_ref, k_ref, v_ref = refs
    else:
        x_ref, mod_ref, g_ref, w_ref, q_ref, k_ref, v_ref = refs
    h = _modulated(x_ref[...], mod_ref[0], g_ref[...], 0, 1).astype(BF16)
    qkv = _dot(h, w_ref[...])
    nq, nkv = q_ref.shape[1], k_ref.shape[1]
    q_scale = B_HEAD_DIM ** -0.5 * LOG2E
    lo32 = (_lane_iota() % 64) < 32
    for c in range(nq // LANES):
        sl = slice(c * LANES, (c + 1) * LANES)
        qc = qkv[:, sl]
        if rope:
            qc = _rope64(qc, cos_ref[...], sin_ref[...], lo32)
        q_ref[:, sl] = (qc * q_scale).astype(BF16)
    for c in range(nkv // LANES):
        sl = slice(c * LANES, (c + 1) * LANES)
        kc = qkv[:, nq + c * LANES: nq + (c + 1) * LANES]
        if rope:
            kc = _rope64(kc, cos_ref[...], sin_ref[...], lo32)
        k_ref[:, sl] = kc.astype(BF16)
    v_ref[...] = qkv[:, nq + nkv:].astype(BF16)


def _proj_c_kernel(rope, *refs):
    if rope:
        (x_ref, mod_ref, g_ref, cos_ref, sin_ref, wd_ref, qg_ref, kvg_ref, wuq_ref, wukv_ref,
         q_ref, k_ref, v_ref) = refs
    else:
        x_ref, mod_ref, g_ref, wd_ref, qg_ref, kvg_ref, wuq_ref, wukv_ref, q_ref, k_ref, v_ref = refs
    h = _modulated(x_ref[...], mod_ref[0], g_ref[...], 0, 1).astype(BF16)
    down = _dot(h, wd_ref[...])
    cq = _rms(down[:, :C_Q_RANK], qg_ref[...]).astype(BF16)
    ckv = _rms(down[:, C_Q_RANK:C_Q_RANK + C_KV_RANK], kvg_ref[...]).astype(BF16)
    k_rope = down[:, C_Q_RANK + C_KV_RANK:]
    q = _dot(cq, wuq_ref[...])
    kv = _dot(ckv, wukv_ref[...])
    q_scale = (C_NOPE_DIM + C_ROPE_DIM) ** -0.5 * LOG2E
    lo32 = (_lane_iota() % 64) < 32
    if rope:
        k_rope = _rope64(k_rope, cos_ref[...], sin_ref[...], lo32)
    k_rope = k_rope.astype(BF16)
    for hd in range(C_HEADS):
        lo = slice(2 * hd * LANES, (2 * hd + 1) * LANES)
        hi = slice((2 * hd + 1) * LANES, (2 * hd + 2) * LANES)
        q_rope = q[:, hi]
        if rope:
            q_rope = _rope64(q_rope, cos_ref[...], sin_ref[...], lo32)
        q_ref[:, lo] = (q[:, lo] * q_scale).astype(BF16)
        q_ref[:, hi] = (q_rope * q_scale).astype(BF16)
        k_ref[:, lo] = kv[:, lo].astype(BF16)
        k_ref[:, hi] = k_rope
        v_ref[:, hd * LANES:(hd + 1) * LANES] = kv[:, hi].astype(BF16)


def _proj_d_kernel(rope, *refs):
    if rope:
        x_ref, mod_ref, g_ref, cos_ref, sin_ref, w_ref, qg_ref, kg_ref, q_ref, k_ref, v_ref = refs
    else:
        x_ref, mod_ref, g_ref, w_ref, qg_ref, kg_ref, q_ref, k_ref, v_ref = refs
    h = _modulated(x_ref[...], mod_ref[0], g_ref[...], 0, 1).astype(BF16)
    qkv = _dot(h, w_ref[...])
    nq, nkv = q_ref.shape[1], k_ref.shape[1]
    q_scale = D_HEAD_DIM ** -0.5 * LOG2E
    for c in range(nq // LANES):
        sl = slice(c * LANES, (c + 1) * LANES)
        qc = _rms(qkv[:, sl], qg_ref[...])
        if rope:
            qc = _rope128(qc, cos_ref[...], sin_ref[...])
        q_ref[:, sl] = (qc * q_scale).astype(BF16)
    for c in range(nkv // LANES):
        sl = slice(c * LANES, (c + 1) * LANES)
        kc = _rms(qkv[:, nq + c * LANES: nq + (c + 1) * LANES], kg_ref[...])
        if rope:
            kc = _rope128(kc, cos_ref[...], sin_ref[...])
        k_ref[:, sl] = kc.astype(BF16)
    v_ref[...] = qkv[:, nq + nkv:].astype(BF16)


def _row_sub_blocks(body, rope, n_out, sub_rows, *refs):
    tm = refs[0].shape[0]
    sub = min(tm, sub_rows)
    n_in = len(refs) - n_out
    per_row = {0} | ({3, 4} if rope else set()) | set(range(n_in, len(refs)))
    for r0 in range(0, tm, sub):
        body(rope, *[r.at[pl.ds(r0, sub)] if i in per_row else r for i, r in enumerate(refs)])


def _project(body, x, mod, mod_row, g, rope_tabs, weights, out_widths, tm, seq, sub_rows):
    rows, d = x.shape
    tiles_per_seq = seq // tm
    const = lambda i: (0, 0)
    in_specs = [pl.BlockSpec((tm, d), lambda i: (i, 0)),
                pl.BlockSpec((1, 6, d), lambda i: (mod_row(i), 0, 0)),
                pl.BlockSpec((1, d), const)]
    args = [x, mod, g.reshape(1, d)]
    if rope_tabs is not None:
        in_specs += [pl.BlockSpec((tm, LANES), lambda i: (i % tiles_per_seq, 0))] * 2
        args += list(rope_tabs)
    for w in weights:
        in_specs.append(pl.BlockSpec(w.shape, const))
        args.append(w)
    return pl.pallas_call(
        functools.partial(_row_sub_blocks, body, rope_tabs is not None, len(out_widths), sub_rows),
        grid=(rows // tm,),
        in_specs=in_specs,
        out_specs=[pl.BlockSpec((tm, w), lambda i: (i, 0)) for w in out_widths],
        out_shape=[jax.ShapeDtypeStruct((rows, w), BF16) for w in out_widths],
        compiler_params=_params(1),
        name=body.__name__.strip("_"),
    )(*args)


def _attn_a_kernel(n_seg, n_units, lambda_init, *refs, gps):
    q_ref = refs[0]
    k_refs = refs[1:1 + n_seg]
    v_refs = refs[1 + n_seg:1 + 2 * n_seg]
    lq1_ref, lk1_ref, lq2_ref, lk2_ref, g_ref, o_ref = refs[1 + 2 * n_seg:]
    lam = (jnp.exp(jnp.sum(lq1_ref[...] * lk1_ref[...], keepdims=True))
           - jnp.exp(jnp.sum(lq2_ref[...] * lk2_ref[...], keepdims=True)) + lambda_init)
    tq = q_ref.shape[1] // n_units
    lane = _lane_iota()
    zero = jnp.zeros((tq, LANES), BF16)
    units = []
    for g in range(gps):
        lanes = slice(g * LANES, (g + 1) * LANES)
        keys, values = _segments(k_refs, KEY_SEG, lanes), _with_ones(_segments(v_refs, KEY_SEG, lanes))
        for u in range(n_units):
            q = q_ref[0, u * tq:(u + 1) * tq, lanes]
            qs = jnp.concatenate([jnp.where(lane < 64, q, zero), jnp.where(lane >= 64, q, zero)], axis=0)
            units.append((qs, keys, values, None, None))
    outs = _attend_units(units)
    for i, o in enumerate(outs):
        g, u = divmod(i, n_units)
        diff = o[:tq] - lam * o[tq:]
        o_ref[0, u * tq:(u + 1) * tq, g * LANES:(g + 1) * LANES] = (
            _rms(diff, g_ref[...]) * (1.0 - lambda_init)).astype(BF16)


def _attn_b_kernel(n_seg, tq, n_lat, *refs, gps):
    q_ref = refs[0]
    k_refs = refs[1:1 + n_seg]
    v_refs = refs[1 + n_seg:1 + 2 * n_seg]
    sink_ref, o_ref = refs[1 + 2 * n_seg:]
    lane = _lane_iota()
    group = B_HEADS // B_KV_HEADS
    n_sub = q_ref.shape[1] // tq
    units = []
    for g in range(gps):
        pair = pl.program_id(1) * gps + g
        lanes = slice(g * LANES, (g + 1) * LANES)
        kc, vc = k_refs[0][0, :, lanes], _with_ones([v_refs[0][0, :, lanes]])[0]
        for t in range(n_sub):
            keys, values, biases = [kc], [vc], None
            if n_seg == 2:
                span = tq + 2 * B_WINDOW
                start = (pl.program_id(2) * n_sub + t) * tq
                k0 = pl.multiple_of(jnp.clip(start - B_WINDOW, 0, n_lat - span), LANES)
                keys.append(k_refs[1][0, pl.ds(k0, span), lanes])
                values += _with_ones([v_refs[1][0, pl.ds(k0, span), lanes]])
                q_pos = start + lax.broadcasted_iota(jnp.int32, (tq, 1), 0)
                k_pos = k0 + lax.broadcasted_iota(jnp.int32, (1, span), 1)
                biases = [None, jnp.where(jnp.abs(q_pos - k_pos) <= B_WINDOW, 0.0, -jnp.inf)]
            rows = slice(t * tq, (t + 1) * tq)
            for half in range(2):
                keep = (lane >= 64) if half else (lane < 64)
                q = jnp.concatenate(
                    [jnp.where(keep, q_ref[0, rows, (g * group + c) * LANES:(g * group + c + 1) * LANES],
                               jnp.zeros((tq, LANES), BF16)) for c in range(group)], axis=0)
                sink = [sink_ref[(2 * pair + half) * group + c] for c in range(group)]
                units.append((q, keys, values, biases, sink))
    outs = _attend_units(units)
    for g in range(gps):
        for t in range(n_sub):
            lo, hi = outs[2 * (g * n_sub + t)], outs[2 * (g * n_sub + t) + 1]
            for c in range(group):
                blk = slice(c * tq, (c + 1) * tq)
                o_ref[0, t * tq:(t + 1) * tq, (g * group + c) * LANES:(g * group + c + 1) * LANES] = (
                    jnp.where(lane < 64, lo[blk], hi[blk]).astype(BF16))


def _attn_plain_kernel(n_seg, n_units, n_heads, dq, dv, *refs, gps):
    q_ref = refs[0]
    k_refs = refs[1:1 + n_seg]
    v_refs = refs[1 + n_seg:1 + 2 * n_seg]
    o_ref = refs[1 + 2 * n_seg]
    tq = q_ref.shape[1] // n_units
    units = []
    for g in range(gps):
        keys = _segments(k_refs, KEY_SEG, slice(g * dq, (g + 1) * dq))
        values = _with_ones(_segments(v_refs, KEY_SEG, slice(g * dv, (g + 1) * dv)))
        for u in range(n_units):
            rows = slice(u * tq, (u + 1) * tq)
            heads = [q_ref[0, rows, (g * n_heads + hd) * dq:(g * n_heads + hd + 1) * dq]
                     for hd in range(n_heads)]
            units.append((heads[0] if n_heads == 1 else jnp.concatenate(heads, axis=0),
                          keys, values, None, None))
    outs = _attend_units(units)
    for i, o in enumerate(outs):
        g, u = divmod(i, n_units)
        for hd in range(n_heads):
            o_ref[0, u * tq:(u + 1) * tq, (g * n_heads + hd) * dv:(g * n_heads + hd + 1) * dv] = (
                o[hd * tq:(hd + 1) * tq].astype(BF16))


def _attention(body, q, ks, vs, extra, extra_specs, n_groups, gps, wq, wk, wv, wo, tq, name):
    b, s, _ = q.shape
    wq, wk, wv, wo = gps * wq, gps * wk, gps * wv, gps * wo
    in_specs = [pl.BlockSpec((1, tq, wq), lambda bi, gi, qi: (bi, qi, gi))]
    for k in ks:
        in_specs.append(pl.BlockSpec((1, k.shape[1], wk), lambda bi, gi, qi: (bi, 0, gi)))
    for v in vs:
        in_specs.append(pl.BlockSpec((1, v.shape[1], wv), lambda bi, gi, qi: (bi, 0, gi)))
    return pl.pallas_call(
        functools.partial(body, gps=gps),
        grid=(b, n_groups // gps, s // tq),
        in_specs=in_specs + extra_specs,
        out_specs=pl.BlockSpec((1, tq, wo), lambda bi, gi, qi: (bi, qi, gi)),
        out_shape=jax.ShapeDtypeStruct((b, s, (n_groups // gps) * wo), BF16),
        compiler_params=_params(3),
        name=name,
    )(q, *ks, *vs, *extra)


def _small_spec(a):
    return pl.BlockSpec(a.shape, lambda bi, gi, qi: (0,) * a.ndim)


def _block_tail_kernel(tiles_per_seq, seq_rows, final, *refs):
    (x_ref, xp_ref, xn_ref, o_ref, op_ref, on_ref, mod_ref, g_ref, wo_ref, wup_ref, cw_ref, cb_ref,
     wdn_ref) = refs[:13]
    fg_ref = refs[13] if final else None
    out_ref, act_ref = refs[-2:]
    tm, d = x_ref.shape
    n_ext = tm + HALO
    pos = pl.program_id(0) % tiles_per_seq
    mod = mod_ref[0]
    r8 = lax.broadcasted_iota(jnp.int32, (HALO, 1), 0)
    r16 = lax.broadcasted_iota(jnp.int32, (O_HALO, 1), 0)
    x_halo = jnp.where(r8 == HALO - 1, xp_ref[...], jnp.where(r8 == 0, xn_ref[...], 0.0))
    o_halo = jnp.where(r16 == O_HALO - 1, op_ref[...],
                       jnp.where(r16 == 0, on_ref[...], jnp.zeros_like(on_ref[...])))
    mixed = _dot(jnp.concatenate([o_halo, o_ref[...]], axis=0), wo_ref[...])
    mixed_halo = jnp.where(r8 == HALO - 1, mixed[HALO:O_HALO], mixed[:HALO])
    x1_ext = (jnp.concatenate([x_halo, x_ref[...]], axis=0)
              + mod[2:3] * jnp.concatenate([mixed_halo, mixed[O_HALO:]], axis=0))
    row = lax.broadcasted_iota(jnp.int32, (n_ext, 1), 0)
    prev_row = jnp.where(pos > 0, HALO - 1, -1)
    next_row = jnp.where(pos < tiles_per_seq - 1, 0, -1)
    keep = (row >= HALO) | (row == prev_row) | (row == next_row)
    h_ext = jnp.where(keep, _modulated(x1_ext, mod, g_ref[...], 3, 4), 0.0).astype(BF16)
    ff = wdn_ref.shape[0]
    if seq_rows is not None:
        in_seq = lax.broadcasted_iota(jnp.int32, (tm, 1), 0) % seq_rows
        seq_start, seq_end = in_seq == 0, in_seq == seq_rows - 1

    def conv_branch(cols):
        u = _dot(h_ext, wup_ref[:, cols])
        u_prev = pltpu.roll(u, 1, 0)[HALO:]
        u_next = pltpu.roll(u, n_ext - 1, 0)[HALO:]
        if seq_rows is not None:
            u_prev = jnp.where(seq_start, 0.0, u_prev)
            u_next = jnp.where(seq_end, 0.0, u_next)
        return (cb_ref[:, cols] + u_prev * cw_ref[0:1, cols] + u[HALO:] * cw_ref[1:2, cols]
                + u_next * cw_ref[2:3, cols])

    for c in range(ff // FF_CHUNK):
        a = conv_branch(slice(c * FF_CHUNK, (c + 1) * FF_CHUNK))
        gate = conv_branch(slice(ff + c * FF_CHUNK, ff + (c + 1) * FF_CHUNK))
        act_ref[:, c * FF_CHUNK:(c + 1) * FF_CHUNK] = ((a / (1.0 + jnp.exp(-a))) * gate).astype(BF16)
    out = x1_ext[HALO:HALO + tm] + mod[5:6] * _dot(act_ref[...], wdn_ref[...])
    if final:
        out = _rms(out, fg_ref[...])
    out_ref[...] = out


def _block_tail(x, o, mod, mod_row, g, w_o, w_up, conv_w, conv_b, w_down, tm, seq, final_g=None):
    rows, d = x.shape
    assert tm % seq == 0 or seq % tm == 0
    tiles_per_seq = max(seq // tm, 1)
    seq_rows = seq if tm > seq else None
    const2 = lambda i: (0, 0)
    once = pl.Buffered(1)

    def halo_specs(width, n_halo):
        per_tile = tm // n_halo
        last = rows // n_halo - 1
        return [pl.BlockSpec((tm, width), lambda i: (i, 0)),
                pl.BlockSpec((n_halo, width), lambda i: (jnp.maximum(i * per_tile - 1, 0), 0)),
                pl.BlockSpec((n_halo, width), lambda i: (jnp.minimum((i + 1) * per_tile, last), 0))]

    in_specs = (halo_specs(d, HALO) + halo_specs(o.shape[1], O_HALO) +
                [pl.BlockSpec((1, 6, d), lambda i: (mod_row(i), 0, 0)),
                 pl.BlockSpec((1, d), const2),
                 pl.BlockSpec(w_o.shape, const2, pipeline_mode=once),
                 pl.BlockSpec(w_up.shape, const2, pipeline_mode=once),
                 pl.BlockSpec(conv_w.shape, const2),
                 pl.BlockSpec(conv_b.shape, const2),
                 pl.BlockSpec(w_down.shape, const2, pipeline_mode=once)])
    args = [x, x, x, o, o, o, mod, g.reshape(1, d), w_o, w_up, conv_w, conv_b, w_down]
    if final_g is not None:
        in_specs.append(pl.BlockSpec((1, d), const2))
        args.append(final_g.reshape(1, d))
    return pl.pallas_call(
        functools.partial(_block_tail_kernel, tiles_per_seq, seq_rows, final_g is not None),
        grid=(rows // tm,),
        in_specs=in_specs,
        out_specs=pl.BlockSpec((tm, d), lambda i: (i, 0)),
        out_shape=jax.ShapeDtypeStruct((rows, d), F32),
        scratch_shapes=[pltpu.VMEM((tm, w_down.shape[0]), BF16)],
        compiler_params=_params(1),
        name="block_tail",
    )(*args)


def _ffn_weights(w_up, conv_w, conv_b, w_down):
    return w_up.astype(BF16), conv_w, conv_b.reshape(1, -1), w_down.astype(BF16)


def kernel(x, c, ctx, c_ctx, ada_w, ada_b, norm1_g, norm2_g, ffn_up, ffn_conv_w, ffn_conv_b, ffn_down,
           a_w_qkv, a_w_o, a_lambda_q1, a_lambda_k1, a_lambda_q2, a_lambda_k2, a_subln_g,
           b_w_qkv, b_w_o, b_sink,
           c_w_down, c_q_norm_g, c_kv_norm_g, c_w_uq, c_w_ukv, c_w_o,
           d_w_qkv, d_q_norm_g, d_k_norm_g, d_w_o,
           final_g):
    batch, seq, d = x.shape
    n_ctx = ctx.shape[1]
    depth = ada_w.shape[0]
    n_mixers = 4

    mods = _ada_mods(c, c_ctx, ada_w, ada_b)
    rope64 = _rope_tables(seq, 64)
    rope128 = _rope_tables(seq, 128)

    tm_lat = min(512, seq)
    tm_proj = min(1024, seq)
    tm_ctx = n_ctx
    packs = tm_lat % n_ctx == 0 and (batch * n_ctx) % tm_lat == 0
    tm_ctx_tail = tm_lat if packs else n_ctx
    stacked = {0: LANES // A_HEAD_DIM, 2: 1, 3: D_HEADS // D_KV_HEADS}
    tq_lat = {k: ATTN_UNITS * ATTN_UNIT_ROWS // n for k, n in stacked.items()}
    tq_lat[1] = 4 * B_SUB_TILE
    tq_ctx = n_ctx
    lat_row = lambda i: i // (seq // tm_lat)
    proj_row = lambda i: i // (seq // tm_proj)
    ctx_row = lambda i: batch

    xl = x.reshape(batch * seq, d)
    xc = ctx.reshape(batch * n_ctx, d)

    def to3(t, n):
        return t.reshape(batch, n, t.shape[-1])

    for i in range(depth):
        kind, r = i % n_mixers, i // n_mixers
        need_ctx = i < depth - 1
        mod = mods[i]
        g1 = norm1_g[i]

        if kind == 0:
            width = 2 * A_HEADS * A_HEAD_DIM
            weights = [a_w_qkv[r].astype(BF16)]
            widths = [width, width, width]
            body, tabs = _proj_a_kernel, rope64
        elif kind == 1:
            nq, nkv = B_HEADS * B_HEAD_DIM, B_KV_HEADS * B_HEAD_DIM
            group = B_HEADS // B_KV_HEADS
            w = b_w_qkv[r]
            w_q = w[:, :nq].reshape(d, B_KV_HEADS // 2, 2, group, B_HEAD_DIM)
            w_q = jnp.swapaxes(w_q, 2, 3).reshape(d, nq)
            weights = [jnp.concatenate([w_q, w[:, nq:]], axis=1).astype(BF16)]
            widths = [nq, nkv, nkv]
            body, tabs = _proj_b_kernel, rope64
        elif kind == 2:
            wd = jnp.pad(c_w_down[r], ((0, 0), (0, LANES - C_ROPE_DIM)))
            wuq = c_w_uq[r].reshape(C_Q_RANK, C_HEADS, C_NOPE_DIM + C_ROPE_DIM)
            wuq = jnp.pad(wuq, ((0, 0), (0, 0), (0, 2 * LANES - C_NOPE_DIM - C_ROPE_DIM)))
            weights = [wd.astype(BF16), c_q_norm_g[r].reshape(1, -1), c_kv_norm_g[r].reshape(1, -1),
                       wuq.reshape(C_Q_RANK, C_HEADS * 2 * LANES).astype(BF16), c_w_ukv[r].astype(BF16)]
            widths = [C_HEADS * 2 * LANES, C_HEADS * 2 * LANES, C_HEADS * C_V_DIM]
            body, tabs = _proj_c_kernel, rope64
        else:
            weights = [d_w_qkv[r].astype(BF16), d_q_norm_g[r].reshape(1, -1), d_k_norm_g[r].reshape(1, -1)]
            widths = [D_HEADS * D_HEAD_DIM, D_KV_HEADS * D_HEAD_DIM, D_KV_HEADS * D_HEAD_DIM]
            body, tabs = _proj_d_kernel, rope128

        ql, kl, vl = _project(body, xl, mod, proj_row, g1, tabs, weights, widths, tm_proj, seq,
                              PROJ_SUB_ROWS[kind])
        qc, kc, vc = _project(body, xc, mod, ctx_row, g1, None, weights, widths, tm_ctx, n_ctx,
                              PROJ_SUB_ROWS[kind])
        ql, kl, vl = to3(ql, seq), to3(kl, seq), to3(vl, seq)
        qc, kc, vc = to3(qc, n_ctx), to3(kc, n_ctx), to3(vc, n_ctx)

        def attend(q, ks, vs, tq, n_units, all_groups):
            n_seg = len(ks)
            if kind == 0:
                lambda_init = 0.8 - 0.6 * math.exp(-0.3 * i)
                extra = [a_lambda_q1[r].reshape(1, -1), a_lambda_k1[r].reshape(1, -1),
                         a_lambda_q2[r].reshape(1, -1), a_lambda_k2[r].reshape(1, -1),
                         a_subln_g[r].reshape(1, -1)]
                return _attention(functools.partial(_attn_a_kernel, n_seg, n_units, lambda_init), q, ks, vs,
                                  extra, [_small_spec(e) for e in extra], A_HEADS,
                                  A_HEADS if all_groups else 1, LANES, LANES, LANES, LANES, tq, "attn_a")
            if kind == 1:
                group = B_HEADS // B_KV_HEADS
                n_pairs = B_KV_HEADS // 2
                return _attention(functools.partial(_attn_b_kernel, n_seg, B_SUB_TILE, seq), q, ks, vs,
                                  [b_sink[r] * LOG2E],
                                  [pl.BlockSpec(memory_space=pltpu.SMEM)], n_pairs,
                                  n_pairs if all_groups else 1,
                                  group * LANES, LANES, LANES, group * LANES, tq, "attn_b")
            if kind == 2:
                return _attention(functools.partial(_attn_plain_kernel, n_seg, n_units, 1, 2 * LANES, C_V_DIM),
                                  q, ks, vs, [], [], C_HEADS, C_HEADS if all_groups else 1,
                                  2 * LANES, 2 * LANES, C_V_DIM, C_V_DIM, tq, "attn_c")
            group = D_HEADS // D_KV_HEADS
            return _attention(functools.partial(_attn_plain_kernel, n_seg, n_units, group, D_HEAD_DIM,
                                                D_HEAD_DIM),
                              q, ks, vs, [], [], D_KV_HEADS, D_KV_HEADS if all_groups else 1,
                              group * D_HEAD_DIM, D_HEAD_DIM, D_HEAD_DIM, group * D_HEAD_DIM, tq, "attn_d")

        if kind == 0:
            w_o = a_w_o[r]
        elif kind == 1:
            w_o = b_w_o[r].reshape(B_KV_HEADS // 2, 2, B_HEADS // B_KV_HEADS, B_HEAD_DIM, d)
            w_o = jnp.swapaxes(w_o, 1, 2).reshape(-1, d)
        elif kind == 2:
            w_o = c_w_o[r]
        else:
            w_o = d_w_o[r]
        w_o = w_o.astype(BF16)
        ffn_w = _ffn_weights(ffn_up[i], ffn_conv_w[i], ffn_conv_b[i], ffn_down[i])

        tq = min(tq_lat[kind], seq)
        n_units = ATTN_UNITS if tq == tq_lat[kind] else 1
        ol = attend(ql, [kc, kl], [vc, vl], tq, n_units, False).reshape(batch * seq, -1)
        xl = _block_tail(xl, ol, mod, lat_row, norm2_g[i], w_o, *ffn_w, tm_lat, seq,
                         final_g=None if need_ctx else final_g)
        if need_ctx:
            oc = attend(qc, [kc], [vc], tq_ctx, 1, True).reshape(batch * n_ctx, -1)
            xc = _block_tail(xc, oc, mod, ctx_row, norm2_g[i], w_o, *ffn_w, tm_ctx_tail, n_ctx)

    return xl.reshape(batch, seq, d)
```

```python
import functools
import math

import jax
import jax.numpy as jnp
from jax import lax
from jax.experimental import pallas as pl
from jax.experimental.pallas import tpu as pltpu

F32 = jnp.float32
BF16 = jnp.bfloat16

GRID_W = 64
ROPE_THETA = 10000.0
NORM_EPS = 1e-6
LOG2E = math.log2(math.e)

A_HEADS, A_HEAD_DIM = 8, 64
B_HEADS, B_KV_HEADS, B_HEAD_DIM, B_WINDOW = 16, 4, 64, 128
C_HEADS, C_Q_RANK, C_KV_RANK, C_NOPE_DIM, C_ROPE_DIM, C_V_DIM = 8, 384, 256, 128, 64, 128
D_HEADS, D_KV_HEADS, D_HEAD_DIM = 8, 4, 128
CONV_W = 3

LANES = 128
SUBLANES = 8
MOD_ROWS = 16
VMEM_LIMIT = 56 * 1024 * 1024
FF_CHUNK = 256
HALO = SUBLANES
O_HALO = 2 * SUBLANES
KEY_SEG = 512
ATTN_UNITS = 4
ATTN_UNIT_ROWS = 512
B_SUB_TILE = 128
PROJ_SUB_ROWS = {0: 512, 1: 512, 2: 512, 3: 256}


def _params(n_axes):
    return pltpu.CompilerParams(dimension_semantics=("arbitrary",) * n_axes,
                                vmem_limit_bytes=VMEM_LIMIT)


def _dot(a, b):
    return jnp.dot(a, b, preferred_element_type=F32)


def _dot_nt(a, b):
    return lax.dot_general(a, b, (((1,), (1,)), ((), ())), preferred_element_type=F32)


def _lane_iota(n=LANES):
    return lax.broadcasted_iota(jnp.int32, (1, n), 1)


def _ada_kernel(c_ref, w_ref, b_ref, o_ref):
    cv = c_ref[...]
    s = cv / (1.0 + jnp.exp(-cv))
    o_ref[0] = _dot(s.astype(BF16), w_ref[0].astype(BF16)) + b_ref[0]


def _ada_mods(c, c_ctx, ada_w, ada_b):
    n_layers, d, n_out = ada_w.shape
    batch = c.shape[0]
    rows = jnp.zeros((MOD_ROWS, d), F32).at[:batch].set(c).at[batch].set(c_ctx)
    tn = 1536
    out = pl.pallas_call(
        _ada_kernel,
        grid=(n_layers, n_out // tn),
        in_specs=[pl.BlockSpec((MOD_ROWS, d), lambda l, j: (0, 0)),
                  pl.BlockSpec((1, d, tn), lambda l, j: (l, 0, j)),
                  pl.BlockSpec((1, 1, tn), lambda l, j: (l, 0, j))],
        out_specs=pl.BlockSpec((1, MOD_ROWS, tn), lambda l, j: (l, 0, j)),
        out_shape=jax.ShapeDtypeStruct((n_layers, MOD_ROWS, n_out), F32),
        compiler_params=_params(2),
        name="ada_mods",
    )(rows, ada_w, ada_b.reshape(n_layers, 1, n_out))
    return out.reshape(n_layers, MOD_ROWS, 6, d)


def _modulated(x, mod, g, shift_row, scale_row):
    ms = jnp.mean(x * x, axis=-1, keepdims=True)
    y = x * lax.rsqrt(ms + NORM_EPS)
    return (y * g) * (1.0 + mod[scale_row:scale_row + 1]) + mod[shift_row:shift_row + 1]


def _rms(x, g):
    ms = jnp.mean(x * x, axis=-1, keepdims=True)
    return x * lax.rsqrt(ms + NORM_EPS) * g


def _rope64(xc, cos, sin, lo32):
    r_dn = pltpu.roll(xc, 32, 1)
    r_up = pltpu.roll(xc, 96, 1)
    return xc * cos + jnp.where(lo32, r_up, r_dn) * sin


def _rope128(xc, cos, sin):
    return xc * cos + pltpu.roll(xc, 64, 1) * sin


def _rope_tables(n_tokens, rot_dim):
    n_rows = n_tokens // GRID_W
    rows = jnp.repeat(jnp.arange(n_rows, dtype=F32), GRID_W)
    cols = jnp.tile(jnp.arange(GRID_W, dtype=F32), n_rows)
    n_freq = rot_dim // 4
    inv_freq = ROPE_THETA ** (-jnp.arange(n_freq, dtype=F32) / n_freq)
    ang = jnp.concatenate([rows[:, None] * inv_freq, cols[:, None] * inv_freq], axis=-1)
    cos, sin = jnp.cos(ang), jnp.sin(ang)
    reps = LANES // rot_dim
    return (jnp.tile(jnp.concatenate([cos, cos], axis=-1), (1, reps)),
            jnp.tile(jnp.concatenate([-sin, sin], axis=-1), (1, reps)))


def _lane_blocks(parts):
    return [s[:, j * LANES:(j + 1) * LANES] for s in parts for j in range(s.shape[1] // LANES)]


def _softmax_pv(scores, values, sink=None):
    m = jnp.max(functools.reduce(jnp.maximum, _lane_blocks(scores)), axis=-1, keepdims=True)
    if sink is not None:
        r = m.shape[0] // len(sink)
        m = jnp.concatenate([jnp.maximum(m[j * r:(j + 1) * r], sj) for j, sj in enumerate(sink)], axis=0)
    o = functools.reduce(jnp.add, [_dot(jnp.exp2(s - m).astype(BF16), v) for s, v in zip(scores, values)])
    l = o[:, LANES:]
    if sink is not None:
        l = l + jnp.concatenate([jnp.exp2(sj - m[j * r:(j + 1) * r]) for j, sj in enumerate(sink)], axis=0)
    return o[:, :LANES] / l


def _add_row_tiled(s, bias):
    r = bias.shape[0]
    return jnp.concatenate([s[j * r:(j + 1) * r] + bias for j in range(s.shape[0] // r)], axis=0)


def _with_ones(values):
    assert all(v.shape[1] == LANES for v in values)
    return [jnp.concatenate([v, jnp.ones_like(v)], axis=1) for v in values]


def _attend_units(units):
    all_scores = []
    for q, keys, _, biases, _ in units:
        scores = [_dot_nt(q, k) for k in keys]
        if biases is not None:
            scores = [s if b is None else _add_row_tiled(s, b) for s, b in zip(scores, biases)]
        all_scores.append(scores)
    return [_softmax_pv(scores, unit[2], unit[4]) for scores, unit in zip(all_scores, units)]


def _segments(refs, seg, lanes):
    out = []
    for r in refs:
        n = r.shape[1]
        if n <= seg:
            out.append(r[0, :, lanes])
        else:
            out += [r[0, j * seg:(j + 1) * seg, lanes] for j in range(n // seg)]
    return out


def _proj_a_kernel(rope, *refs):
    if rope:
        x_ref, mod_ref, g_ref, cos_ref, sin_ref, w_ref, q_ref, k_ref, v_ref = refs
    else:
        x_ref, mod_ref, g_ref, w_ref, q_ref, k_ref, v_ref = refs
    h = _modulated(x_ref[...], mod_ref[0], g_ref[...], 0, 1).astype(BF16)
    qkv = _dot(h, w_ref[...])
    width = q_ref.shape[1]
    q_scale = A_HEAD_DIM ** -0.5 * LOG2E
    lo32 = (_lane_iota() % 64) < 32
    for c in range(width // LANES):
        sl = slice(c * LANES, (c + 1) * LANES)
        qc = qkv[:, sl]
        kc = qkv[:, width + c * LANES: width + (c + 1) * LANES]
        if rope:
            qc = _rope64(qc, cos_ref[...], sin_ref[...], lo32)
            kc = _rope64(kc, cos_ref[...], sin_ref[...], lo32)
        q_ref[:, sl] = (qc * q_scale).astype(BF16)
        k_ref[:, sl] = kc.astype(BF16)
    v_ref[...] = qkv[:, 2 * width:].astype(BF16)


def _proj_b_kernel(rope, *refs):
    if rope:
        x_ref, mod_ref, g_ref, cos_ref, sin_ref, w_ref, q_ref, k_ref, v_ref = refs
    else:
        x_ref, mod_ref, g_ref, w_ref, q_ref, k_ref, v_ref = refs
    h = _modulated(x_ref[...], mod_ref[0], g_ref[...], 0, 1).astype(BF16)
    qkv = _dot(h, w_ref[...])
    nq, nkv = q_ref.shape[1], k_ref.shape[1]
    q_scale = B_HEAD_DIM ** -0.5 * LOG2E
    lo32 = (_lane_iota() % 64) < 32
    for c in range(nq // LANES):
        sl = slice(c * LANES, (c + 1) * LANES)
        qc = qkv[:, sl]
        if rope:
            qc = _rope64(qc, cos_ref[...], sin_ref[...], lo32)
        q_ref[:, sl] = (qc * q_scale).astype(BF16)
    for c in range(nkv // LANES):
        sl = slice(c * LANES, (c + 1) * LANES)
        kc = qkv[:, nq + c * LANES: nq + (c + 1) * LANES]
        if rope:
            kc = _rope64(kc, cos_ref[...], sin_ref[...], lo32)
        k_ref[:, sl] = kc.astype(BF16)
    v_ref[...] = qkv[:, nq + nkv:].astype(BF16)


def _proj_c_kernel(rope, *refs):
    if rope:
        (x_ref, mod_ref, g_ref, cos_ref, sin_ref, wd_ref, qg_ref, kvg_ref, wuq_ref, wukv_ref,
         q_ref, k_ref, v_ref) = refs
    else:
        x_ref, mod_ref, g_ref, wd_ref, qg_ref, kvg_ref, wuq_ref, wukv_ref, q_ref, k_ref, v_ref = refs
    h = _modulated(x_ref[...], mod_ref[0], g_ref[...], 0, 1).astype(BF16)
    down = _dot(h, wd_ref[...])
    cq = _rms(down[:, :C_Q_RANK], qg_ref[...]).astype(BF16)
    ckv = _rms(down[:, C_Q_RANK:C_Q_RANK + C_KV_RANK], kvg_ref[...]).astype(BF16)
    k_rope = down[:, C_Q_RANK + C_KV_RANK:]
    q = _dot(cq, wuq_ref[...])
    kv = _dot(ckv, wukv_ref[...])
    q_scale = (C_NOPE_DIM + C_ROPE_DIM) ** -0.5 * LOG2E
    lo32 = (_lane_iota() % 64) < 32
    if rope:
        k_rope = _rope64(k_rope, cos_ref[...], sin_ref[...], lo32)
    k_rope = k_rope.astype(BF16)
    for hd in range(C_HEADS):
        lo = slice(2 * hd * LANES, (2 * hd + 1) * LANES)
        hi = slice((2 * hd + 1) * LANES, (2 * hd + 2) * LANES)
        q_rope = q[:, hi]
        if rope:
            q_rope = _rope64(q_rope, cos_ref[...], sin_ref[...], lo32)
        q_ref[:, lo] = (q[:, lo] * q_scale).astype(BF16)
        q_ref[:, hi] = (q_rope * q_scale).astype(BF16)
        k_ref[:, lo] = kv[:, lo].astype(BF16)
        k_ref[:, hi] = k_rope
        v_ref[:, hd * LANES:(hd + 1) * LANES] = kv[:, hi].astype(BF16)


def _proj_d_kernel(rope, *refs):
    if rope:
        x_ref, mod_ref, g_ref, cos_ref, sin_ref, w_ref, qg_ref, kg_ref, q_ref, k_ref, v_ref = refs
    else:
        x_ref, mod_ref, g_ref, w_ref, qg_ref, kg_ref, q_ref, k_ref, v_ref = refs
    h = _modulated(x_ref[...], mod_ref[0], g_ref[...], 0, 1).astype(BF16)
    qkv = _dot(h, w_ref[...])
    nq, nkv = q_ref.shape[1], k_ref.shape[1]
    q_scale = D_HEAD_DIM ** -0.5 * LOG2E
    for c in range(nq // LANES):
        sl = slice(c * LANES, (c + 1) * LANES)
        qc = _rms(qkv[:, sl], qg_ref[...])
        if rope:
            qc = _rope128(qc, cos_ref[...], sin_ref[...])
        q_ref[:, sl] = (qc * q_scale).astype(BF16)
    for c in range(nkv // LANES):
        sl = slice(c * LANES, (c + 1) * LANES)
        kc = _rms(qkv[:, nq + c * LANES: nq + (c + 1) * LANES], kg_ref[...])
        if rope:
            kc = _rope128(kc, cos_ref[...], sin_ref[...])
        k_ref[:, sl] = kc.astype(BF16)
    v_ref[...] = qkv[:, nq + nkv:].astype(BF16)


def _row_sub_blocks(body, rope, n_out, sub_rows, *refs):
    tm = refs[0].shape[0]
    sub = min(tm, sub_rows)
    n_in = len(refs) - n_out
    per_row = {0} | ({3, 4} if rope else set()) | set(range(n_in, len(refs)))
    for r0 in range(0, tm, sub):
        body(rope, *[r.at[pl.ds(r0, sub)] if i in per_row else r for i, r in enumerate(refs)])


def _project(body, x, mod, mod_row, g, rope_tabs, weights, out_widths, tm, seq, sub_rows):
    rows, d = x.shape
    tiles_per_seq = seq // tm
    const = lambda i: (0, 0)
    in_specs = [pl.BlockSpec((tm, d), lambda i: (i, 0)),
                pl.BlockSpec((1, 6, d), lambda i: (mod_row(i), 0, 0)),
                pl.BlockSpec((1, d), const)]
    args = [x, mod, g.reshape(1, d)]
    if rope_tabs is not None:
        in_specs += [pl.BlockSpec((tm, LANES), lambda i: (i % tiles_per_seq, 0))] * 2
        args += list(rope_tabs)
    for w in weights:
        in_specs.append(pl.BlockSpec(w.shape, const))
        args.append(w)
    return pl.pallas_call(
        functools.partial(_row_sub_blocks, body, rope_tabs is not None, len(out_widths), sub_rows),
        grid=(rows // tm,),
        in_specs=in_specs,
        out_specs=[pl.BlockSpec((tm, w), lambda i: (i, 0)) for w in out_widths],
        out_shape=[jax.ShapeDtypeStruct((rows, w), BF16) for w in out_widths],
        compiler_params=_params(1),
        name=body.__name__.strip("_"),
    )(*args)


def _attn_a_kernel(n_seg, n_units, lambda_init, *refs, gps):
    q_ref = refs[0]
    k_refs = refs[1:1 + n_seg]
    v_refs = refs[1 + n_seg:1 + 2 * n_seg]
    lq1_ref, lk1_ref, lq2_ref, lk2_ref, g_ref, o_ref = refs[1 + 2 * n_seg:]
    lam = (jnp.exp(jnp.sum(lq1_ref[...] * lk1_ref[...], keepdims=True))
           - jnp.exp(jnp.sum(lq2_ref[...] * lk2_ref[...], keepdims=True)) + lambda_init)
    tq = q_ref.shape[1] // n_units
    lane = _lane_iota()
    zero = jnp.zeros((tq, LANES), BF16)
    units = []
    for g in range(gps):
        lanes = slice(g * LANES, (g + 1) * LANES)
        keys, values = _segments(k_refs, KEY_SEG, lanes), _with_ones(_segments(v_refs, KEY_SEG, lanes))
        for u in range(n_units):
            q = q_ref[0, u * tq:(u + 1) * tq, lanes]
            qs = jnp.concatenate([jnp.where(lane < 64, q, zero), jnp.where(lane >= 64, q, zero)], axis=0)
            units.append((qs, keys, values, None, None))
    outs = _attend_units(units)
    for i, o in enumerate(outs):
        g, u = divmod(i, n_units)
        diff = o[:tq] - lam * o[tq:]
        o_ref[0, u * tq:(u + 1) * tq, g * LANES:(g + 1) * LANES] = (
            _rms(diff, g_ref[...]) * (1.0 - lambda_init)).astype(BF16)


def _attn_b_kernel(n_seg, tq, n_lat, *refs, gps):
    q_ref = refs[0]
    k_refs = refs[1:1 + n_seg]
    v_refs = refs[1 + n_seg:1 + 2 * n_seg]
    sink_ref, o_ref = refs[1 + 2 * n_seg:]
    lane = _lane_iota()
    group = B_HEADS // B_KV_HEADS
    n_sub = q_ref.shape[1] // tq
    units = []
    for g in range(gps):
        pair = pl.program_id(1) * gps + g
        lanes = slice(g * LANES, (g + 1) * LANES)
        kc, vc = k_refs[0][0, :, lanes], _with_ones([v_refs[0][0, :, lanes]])[0]
        for t in range(n_sub):
            keys, values, biases = [kc], [vc], None
            if n_seg == 2:
                span = tq + 2 * B_WINDOW
                start = (pl.program_id(2) * n_sub + t) * tq
                k0 = pl.multiple_of(jnp.clip(start - B_WINDOW, 0, n_lat - span), LANES)
                keys.append(k_refs[1][0, pl.ds(k0, span), lanes])
                values += _with_ones([v_refs[1][0, pl.ds(k0, span), lanes]])
                q_pos = start + lax.broadcasted_iota(jnp.int32, (tq, 1), 0)
                k_pos = k0 + lax.broadcasted_iota(jnp.int32, (1, span), 1)
                biases = [None, jnp.where(jnp.abs(q_pos - k_pos) <= B_WINDOW, 0.0, -jnp.inf)]
            rows = slice(t * tq, (t + 1) * tq)
            for half in range(2):
                keep = (lane >= 64) if half else (lane < 64)
                q = jnp.concatenate(
                    [jnp.where(keep, q_ref[0, rows, (g * group + c) * LANES:(g * group + c + 1) * LANES],
                               jnp.zeros((tq, LANES), BF16)) for c in range(group)], axis=0)
                sink = [sink_ref[(2 * pair + half) * group + c] for c in range(group)]
                units.append((q, keys, values, biases, sink))
    outs = _attend_units(units)
    for g in range(gps):
        for t in range(n_sub):
            lo, hi = outs[2 * (g * n_sub + t)], outs[2 * (g * n_sub + t) + 1]
            for c in range(group):
                blk = slice(c * tq, (c + 1) * tq)
                o_ref[0, t * tq:(t + 1) * tq, (g * group + c) * LANES:(g * group + c + 1) * LANES] = (
                    jnp.where(lane < 64, lo[blk], hi[blk]).astype(BF16))


def _attn_plain_kernel(n_seg, n_units, n_heads, dq, dv, *refs, gps):
    q_ref = refs[0]
    k_refs = refs[1:1 + n_seg]
    v_refs = refs[1 + n_seg:1 + 2 * n_seg]
    o_ref = refs[1 + 2 * n_seg]
    tq = q_ref.shape[1] // n_units
    units = []
    for g in range(gps):
        keys = _segments(k_refs, KEY_SEG, slice(g * dq, (g + 1) * dq))
        values = _with_ones(_segments(v_refs, KEY_SEG, slice(g * dv, (g + 1) * dv)))
        for u in range(n_units):
            rows = slice(u * tq, (u + 1) * tq)
            heads = [q_ref[0, rows, (g * n_heads + hd) * dq:(g * n_heads + hd + 1) * dq]
                     for hd in range(n_heads)]
            units.append((heads[0] if n_heads == 1 else jnp.concatenate(heads, axis=0),
                          keys, values, None, None))
    outs = _attend_units(units)
    for i, o in enumerate(outs):
        g, u = divmod(i, n_units)
        for hd in range(n_heads):
            o_ref[0, u * tq:(u + 1) * tq, (g * n_heads + hd) * dv:(g * n_heads + hd + 1) * dv] = (
                o[hd * tq:(hd + 1) * tq].astype(BF16))


def _attention(body, q, ks, vs, extra, extra_specs, n_groups, gps, wq, wk, wv, wo, tq, name):
    b, s, _ = q.shape
    wq, wk, wv, wo = gps * wq, gps * wk, gps * wv, gps * wo
    in_specs = [pl.BlockSpec((1, tq, wq), lambda bi, gi, qi: (bi, qi, gi))]
    for k in ks:
        in_specs.append(pl.BlockSpec((1, k.shape[1], wk), lambda bi, gi, qi: (bi, 0, gi)))
    for v in vs:
        in_specs.append(pl.BlockSpec((1, v.shape[1], wv), lambda bi, gi, qi: (bi, 0, gi)))
    return pl.pallas_call(
        functools.partial(body, gps=gps),
        grid=(b, n_groups // gps, s // tq),
        in_specs=in_specs + extra_specs,
        out_specs=pl.BlockSpec((1, tq, wo), lambda bi, gi, qi: (bi, qi, gi)),
        out_shape=jax.ShapeDtypeStruct((b, s, (n_groups // gps) * wo), BF16),
        compiler_params=_params(3),
        name=name,
    )(q, *ks, *vs, *extra)


def _small_spec(a):
    return pl.BlockSpec(a.shape, lambda bi, gi, qi: (0,) * a.ndim)


def _block_tail_kernel(tiles_per_seq, seq_rows, final, *refs):
    (x_ref, xp_ref, xn_ref, o_ref, op_ref, on_ref, mod_ref, g_ref, wo_ref, wup_ref, cw_ref, cb_ref,
     wdn_ref) = refs[:13]
    fg_ref = refs[13] if final else None
    out_ref, act_ref = refs[-2:]
    tm, d = x_ref.shape
    n_ext = tm + HALO
    pos = pl.program_id(0) % tiles_per_seq
    mod = mod_ref[0]
    r8 = lax.broadcasted_iota(jnp.int32, (HALO, 1), 0)
    r16 = lax.broadcasted_iota(jnp.int32, (O_HALO, 1), 0)
    x_halo = jnp.where(r8 == HALO - 1, xp_ref[...], jnp.where(r8 == 0, xn_ref[...], 0.0))
    o_halo = jnp.where(r16 == O_HALO - 1, op_ref[...],
                       jnp.where(r16 == 0, on_ref[...], jnp.zeros_like(on_ref[...])))
    mixed = _dot(jnp.concatenate([o_halo, o_ref[...]], axis=0), wo_ref[...])
    mixed_halo = jnp.where(r8 == HALO - 1, mixed[HALO:O_HALO], mixed[:HALO])
    x1_ext = (jnp.concatenate([x_halo, x_ref[...]], axis=0)
              + mod[2:3] * jnp.concatenate([mixed_halo, mixed[O_HALO:]], axis=0))
    row = lax.broadcasted_iota(jnp.int32, (n_ext, 1), 0)
    prev_row = jnp.where(pos > 0, HALO - 1, -1)
    next_row = jnp.where(pos < tiles_per_seq - 1, 0, -1)
    keep = (row >= HALO) | (row == prev_row) | (row == next_row)
    h_ext = jnp.where(keep, _modulated(x1_ext, mod, g_ref[...], 3, 4), 0.0).astype(BF16)
    ff = wdn_ref.shape[0]
    if seq_rows is not None:
        in_seq = lax.broadcasted_iota(jnp.int32, (tm, 1), 0) % seq_rows
        seq_start, seq_end = in_seq == 0, in_seq == seq_rows - 1

    def conv_branch(cols):
        u = _dot(h_ext, wup_ref[:, cols])
        u_prev = pltpu.roll(u, 1, 0)[HALO:]
        u_next = pltpu.roll(u, n_ext - 1, 0)[HALO:]
        if seq_rows is not None:
            u_prev = jnp.where(seq_start, 0.0, u_prev)
            u_next = jnp.where(seq_end, 0.0, u_next)
        return (cb_ref[:, cols] + u_prev * cw_ref[0:1, cols] + u[HALO:] * cw_ref[1:2, cols]
                + u_next * cw_ref[2:3, cols])

    for c in range(ff // FF_CHUNK):
        a = conv_branch(slice(c * FF_CHUNK, (c + 1) * FF_CHUNK))
        gate = conv_branch(slice(ff + c * FF_CHUNK, ff + (c + 1) * FF_CHUNK))
        act_ref[:, c * FF_CHUNK:(c + 1) * FF_CHUNK] = ((a / (1.0 + jnp.exp(-a))) * gate).astype(BF16)
    out = x1_ext[HALO:HALO + tm] + mod[5:6] * _dot(act_ref[...], wdn_ref[...])
    if final:
        out = _rms(out, fg_ref[...])
    out_ref[...] = out


def _block_tail(x, o, mod, mod_row, g, w_o, w_up, conv_w, conv_b, w_down, tm, seq, final_g=None):
    rows, d = x.shape
    assert tm % seq == 0 or seq % tm == 0
    tiles_per_seq = max(seq // tm, 1)
    seq_rows = seq if tm > seq else None
    const2 = lambda i: (0, 0)
    once = pl.Buffered(1)

    def halo_specs(width, n_halo):
        per_tile = tm // n_halo
        last = rows // n_halo - 1
        return [pl.BlockSpec((tm, width), lambda i: (i, 0)),
                pl.BlockSpec((n_halo, width), lambda i: (jnp.maximum(i * per_tile - 1, 0), 0)),
                pl.BlockSpec((n_halo, width), lambda i: (jnp.minimum((i + 1) * per_tile, last), 0))]

    in_specs = (halo_specs(d, HALO) + halo_specs(o.shape[1], O_HALO) +
                [pl.BlockSpec((1, 6, d), lambda i: (mod_row(i), 0, 0)),
                 pl.BlockSpec((1, d), const2),
                 pl.BlockSpec(w_o.shape, const2, pipeline_mode=once),
                 pl.BlockSpec(w_up.shape, const2, pipeline_mode=once),
                 pl.BlockSpec(conv_w.shape, const2),
                 pl.BlockSpec(conv_b.shape, const2),
                 pl.BlockSpec(w_down.shape, const2, pipeline_mode=once)])
    args = [x, x, x, o, o, o, mod, g.reshape(1, d), w_o, w_up, conv_w, conv_b, w_down]
    if final_g is not None:
        in_specs.append(pl.BlockSpec((1, d), const2))
        args.append(final_g.reshape(1, d))
    return pl.pallas_call(
        functools.partial(_block_tail_kernel, tiles_per_seq, seq_rows, final_g is not None),
        grid=(rows // tm,),
        in_specs=in_specs,
        out_specs=pl.BlockSpec((tm, d), lambda i: (i, 0)),
        out_shape=jax.ShapeDtypeStruct((rows, d), F32),
        scratch_shapes=[pltpu.VMEM((tm, w_down.shape[0]), BF16)],
        compiler_params=_params(1),
        name="block_tail",
    )(*args)


def _ffn_weights(w_up, conv_w, conv_b, w_down):
    return w_up.astype(BF16), conv_w, conv_b.reshape(1, -1), w_down.astype(BF16)


def kernel(x, c, ctx, c_ctx, ada_w, ada_b, norm1_g, norm2_g, ffn_up, ffn_conv_w, ffn_conv_b, ffn_down,
           a_w_qkv, a_w_o, a_lambda_q1, a_lambda_k1, a_lambda_q2, a_lambda_k2, a_subln_g,
           b_w_qkv, b_w_o, b_sink,
           c_w_down, c_q_norm_g, c_kv_norm_g, c_w_uq, c_w_ukv, c_w_o,
           d_w_qkv, d_q_norm_g, d_k_norm_g, d_w_o,
           final_g):
    batch, seq, d = x.shape
    n_ctx = ctx.shape[1]
    depth = ada_w.shape[0]
    n_mixers = 4

    mods = _ada_mods(c, c_ctx, ada_w, ada_b)
    rope64 = _rope_tables(seq, 64)
    rope128 = _rope_tables(seq, 128)

    tm_lat = min(512, seq)
    tm_proj = min(1024, seq)
    packs = tm_lat % n_ctx == 0 and (batch * n_ctx) % tm_lat == 0
    tm_ctx = tm_lat if packs else n_ctx
    stacked = {0: LANES // A_HEAD_DIM, 2: 1, 3: D_HEADS // D_KV_HEADS}
    tq_lat = {k: ATTN_UNITS * ATTN_UNIT_ROWS // n for k, n in stacked.items()}
    tq_lat[1] = 4 * B_SUB_TILE
    tq_ctx = n_ctx
    lat_row = lambda i: i // (seq // tm_lat)
    proj_row = lambda i: i // (seq // tm_proj)
    ctx_row = lambda i: batch

    xl = x.reshape(batch * seq, d)
    xc = ctx.reshape(batch * n_ctx, d)

    def to3(t, n):
        return t.reshape(batch, n, t.shape[-1])

    for i in range(depth):
        kind, r = i % n_mixers, i // n_mixers
        need_ctx = i < depth - 1
        mod = mods[i]
        g1 = norm1_g[i]

        if kind == 0:
            width = 2 * A_HEADS * A_HEAD_DIM
            weights = [a_w_qkv[r].astype(BF16)]
            widths = [width, width, width]
            body, tabs = _proj_a_kernel, rope64
        elif kind == 1:
            nq, nkv = B_HEADS * B_HEAD_DIM, B_KV_HEADS * B_HEAD_DIM
            group = B_HEADS // B_KV_HEADS
            w = b_w_qkv[r]
            w_q = w[:, :nq].reshape(d, B_KV_HEADS // 2, 2, group, B_HEAD_DIM)
            w_q = jnp.swapaxes(w_q, 2, 3).reshape(d, nq)
            weights = [jnp.concatenate([w_q, w[:, nq:]], axis=1).astype(BF16)]
            widths = [nq, nkv, nkv]
            body, tabs = _proj_b_kernel, rope64
        elif kind == 2:
            wd = jnp.pad(c_w_down[r], ((0, 0), (0, LANES - C_ROPE_DIM)))
            wuq = c_w_uq[r].reshape(C_Q_RANK, C_HEADS, C_NOPE_DIM + C_ROPE_DIM)
            wuq = jnp.pad(wuq, ((0, 0), (0, 0), (0, 2 * LANES - C_NOPE_DIM - C_ROPE_DIM)))
            weights = [wd.astype(BF16), c_q_norm_g[r].reshape(1, -1), c_kv_norm_g[r].reshape(1, -1),
                       wuq.reshape(C_Q_RANK, C_HEADS * 2 * LANES).astype(BF16), c_w_ukv[r].astype(BF16)]
            widths = [C_HEADS * 2 * LANES, C_HEADS * 2 * LANES, C_HEADS * C_V_DIM]
            body, tabs = _proj_c_kernel, rope64
        else:
            weights = [d_w_qkv[r].astype(BF16), d_q_norm_g[r].reshape(1, -1), d_k_norm_g[r].reshape(1, -1)]
            widths = [D_HEADS * D_HEAD_DIM, D_KV_HEADS * D_HEAD_DIM, D_KV_HEADS * D_HEAD_DIM]
            body, tabs = _proj_d_kernel, rope128

        ql, kl, vl = _project(body, xl, mod, proj_row, g1, tabs, weights, widths, tm_proj, seq,
                              PROJ_SUB_ROWS[kind])
        qc, kc, vc = _project(body, xc, mod, ctx_row, g1, None, weights, widths, tm_ctx, n_ctx,
                              PROJ_SUB_ROWS[kind])
        ql, kl, vl = to3(ql, seq), to3(kl, seq), to3(vl, seq)
        qc, kc, vc = to3(qc, n_ctx), to3(kc, n_ctx), to3(vc, n_ctx)

        def attend(q, ks, vs, tq, n_units, all_groups):
            n_seg = len(ks)
            if kind == 0:
                lambda_init = 0.8 - 0.6 * math.exp(-0.3 * i)
                extra = [a_lambda_q1[r].reshape(1, -1), a_lambda_k1[r].reshape(1, -1),
                         a_lambda_q2[r].reshape(1, -1), a_lambda_k2[r].reshape(1, -1),
                         a_subln_g[r].reshape(1, -1)]
                return _attention(functools.partial(_attn_a_kernel, n_seg, n_units, lambda_init), q, ks, vs,
                                  extra, [_small_spec(e) for e in extra], A_HEADS,
                                  A_HEADS if all_groups else 1, LANES, LANES, LANES, LANES, tq, "attn_a")
            if kind == 1:
                group = B_HEADS // B_KV_HEADS
                n_pairs = B_KV_HEADS // 2
                return _attention(functools.partial(_attn_b_kernel, n_seg, B_SUB_TILE, seq), q, ks, vs,
                                  [b_sink[r] * LOG2E],
                                  [pl.BlockSpec(memory_space=pltpu.SMEM)], n_pairs,
                                  n_pairs if all_groups else 1,
                                  group * LANES, LANES, LANES, group * LANES, tq, "attn_b")
            if kind == 2:
                return _attention(functools.partial(_attn_plain_kernel, n_seg, n_units, 1, 2 * LANES, C_V_DIM),
                                  q, ks, vs, [], [], C_HEADS, C_HEADS if all_groups else 1,
                                  2 * LANES, 2 * LANES, C_V_DIM, C_V_DIM, tq, "attn_c")
            group = D_HEADS // D_KV_HEADS
            return _attention(functools.partial(_attn_plain_kernel, n_seg, n_units, group, D_HEAD_DIM,
                                                D_HEAD_DIM),
                              q, ks, vs, [], [], D_KV_HEADS, D_KV_HEADS if all_groups else 1,
                              group * D_HEAD_DIM, D_HEAD_DIM, D_HEAD_DIM, group * D_HEAD_DIM, tq, "attn_d")

        if kind == 0:
            w_o = a_w_o[r]
        elif kind == 1:
            w_o = b_w_o[r].reshape(B_KV_HEADS // 2, 2, B_HEADS // B_KV_HEADS, B_HEAD_DIM, d)
            w_o = jnp.swapaxes(w_o, 1, 2).reshape(-1, d)
        elif kind == 2:
            w_o = c_w_o[r]
        else:
            w_o = d_w_o[r]
        w_o = w_o.astype(BF16)
        ffn_w = _ffn_weights(ffn_up[i], ffn_conv_w[i], ffn_conv_b[i], ffn_down[i])

        tq = min(tq_lat[kind], seq)
        n_units = ATTN_UNITS if tq == tq_lat[kind] else 1
        ol = attend(ql, [kc, kl], [vc, vl], tq, n_units, False).reshape(batch * seq, -1)
        xl = _block_tail(xl, ol, mod, lat_row, norm2_g[i], w_o, *ffn_w, tm_lat, seq,
                         final_g=None if need_ctx else final_g)
        if need_ctx:
            oc = attend(qc, [kc], [vc], tq_ctx, 1, True).reshape(batch * n_ctx, -1)
            xc = _block_tail(xc, oc, mod, ctx_row, norm2_g[i], w_o, *ffn_w, tm_ctx, n_ctx)

    return xl.reshape(batch, seq, d)
```
